```python
import math
import jax
import jax.numpy as jnp
from jax import lax
import numpy as np

D_MODEL = 1024
BATCH = 8
SEQ = 4096
DEPTH = 4

GRID_W = 64
CTX_LEN = 256
N_MIXERS = 2
WIDTH = D_MODEL
NA_HEADS = 16
NA_HEAD_DIM = WIDTH // NA_HEADS
NA_WIN_R = 8
NA_WIN_C = 16
DIFF_HEADS = 8
DIFF_HEAD_DIM = WIDTH // (2 * DIFF_HEADS)
Q_BLOCK = 128
ROPE_BASE = 10000.0
LN_EPS = 1e-5
DEEPNORM_ALPHA = (2.0 * DEPTH) ** 0.25
DEEPNORM_BETA = (8.0 * DEPTH) ** -0.25
N_NA_LAYERS = (DEPTH + 1) // 2
N_DIFF_LAYERS = DEPTH // 2

kernel_name = 'hybrid_natten_diffattn_dit'


def _layer_norm(x, g, b):
    xf = x.astype(jnp.float32)
    mu = jnp.mean(xf, axis=-1, keepdims=True)
    var = jnp.mean(jnp.square(xf - mu), axis=-1, keepdims=True)
    return ((xf - mu) * lax.rsqrt(var + LN_EPS)).astype(x.dtype) * g + b


def _rms_norm(x, g):
    xf = x.astype(jnp.float32)
    ms = jnp.mean(jnp.square(xf), axis=-1, keepdims=True)
    return (xf * lax.rsqrt(ms + LN_EPS)).astype(x.dtype) * g


def _modulation(cond, w_mod, b_mod):
    m = jax.nn.silu(cond) @ w_mod + b_mod
    return jnp.split(m, 3, axis=-1)


def _split_heads(t, n_heads):
    b, n, _ = t.shape
    return t.reshape(b, n, n_heads, -1).transpose(0, 2, 1, 3)


def _merge_heads(t):
    b, h, n, d = t.shape
    return t.transpose(0, 2, 1, 3).reshape(b, n, h * d)


def _softmax_attend(q, k, v, scale):
    s = jnp.einsum('bhqd,bhkd->bhqk', q, k).astype(jnp.float32) * scale
    p = jax.nn.softmax(s, axis=-1).astype(v.dtype)
    return jnp.einsum('bhqk,bhkd->bhqd', p, v)


def _axial_rope_tables(n_tok):
    t = jnp.arange(n_tok, dtype=jnp.int32)
    pos = jnp.stack([t // GRID_W, t % GRID_W], axis=-1).astype(jnp.float32)
    n_freq = DIFF_HEAD_DIM // 4
    inv_freq = ROPE_BASE ** (-jnp.arange(n_freq, dtype=jnp.float32) / n_freq)
    ang = pos[:, :, None] * inv_freq
    return jnp.cos(ang), jnp.sin(ang)


def _apply_axial_rope(x, cos, sin):
    shp = x.shape
    xr = x.reshape(shp[:-1] + (2, 2, shp[-1] // 4))
    x1, x2 = xr[..., 0, :], xr[..., 1, :]
    cos = cos.astype(x.dtype)
    sin = sin.astype(x.dtype)
    out = jnp.stack([x1 * cos - x2 * sin, x2 * cos + x1 * sin], axis=-2)
    return out.reshape(shp)


def _na_mixer(q, k, v, qc, kc, vc, rpb, need_ctx):
    q, k, v = _split_heads(q, NA_HEADS), _split_heads(k, NA_HEADS), _split_heads(v, NA_HEADS)
    kc, vc = _split_heads(kc, NA_HEADS), _split_heads(vc, NA_HEADS)
    b, h, n, d = q.shape
    rows = n // GRID_W
    kr = min(NA_WIN_R, rows)
    n_win = kr * NA_WIN_C
    scale = d ** -0.5
    kg = k.reshape(b, h, rows, GRID_W, d)
    vg = v.reshape(b, h, rows, GRID_W, d)
    q_rows = jnp.moveaxis(q.reshape(b, h, rows, GRID_W, d), 2, 0)
    r_idx = jnp.arange(rows, dtype=jnp.int32)
    row_start = jnp.clip(r_idx - kr // 2, 0, rows - kr)
    w_idx = jnp.arange(GRID_W, dtype=jnp.int32)
    col_start = jnp.clip(w_idx - NA_WIN_C // 2, 0, GRID_W - NA_WIN_C)
    col_idx = col_start[:, None] + jnp.arange(NA_WIN_C, dtype=jnp.int32)
    rpb_cols = rpb[:, :, col_idx - w_idx[:, None] + NA_WIN_C - 1]

    def row_block(args):
        q_r, r, r0 = args
        k_win = lax.dynamic_slice_in_dim(kg, r0, kr, axis=2)[:, :, :, col_idx, :]
        v_win = lax.dynamic_slice_in_dim(vg, r0, kr, axis=2)[:, :, :, col_idx, :]
        bias = rpb_cols[:, r0 + jnp.arange(kr, dtype=jnp.int32) - r + NA_WIN_R - 1]
        s_loc = (jnp.einsum('bhwd,bhiwjd->bhwij', q_r, k_win).astype(jnp.float32) * scale
                 + bias.transpose(0, 2, 1, 3).astype(jnp.float32))
        s_ctx = jnp.einsum('bhwd,bhcd->bhwc', q_r, kc).astype(jnp.float32) * scale
        s = jnp.concatenate([s_loc.reshape(b, h, GRID_W, n_win), s_ctx], axis=-1)
        p = jax.nn.softmax(s, axis=-1).astype(v.dtype)
        p_loc = p[..., :n_win].reshape(b, h, GRID_W, kr, NA_WIN_C)
        return (jnp.einsum('bhwij,bhiwjd->bhwd', p_loc, v_win)
                + jnp.einsum('bhwc,bhcd->bhwd', p[..., n_win:], vc))

    y = lax.map(row_block, (q_rows, r_idx, row_start))
    y = jnp.moveaxis(y, 0, 2).reshape(b, h, n, d)
    y_ctx = None
    if need_ctx:
        y_ctx = _merge_heads(_softmax_attend(_split_heads(qc, NA_HEADS), kc, vc, scale))
    return _merge_heads(y), y_ctx


def _diff_heads(t):
    b, n, _ = t.shape
    return t.reshape(b, n, DIFF_HEADS, 2, DIFF_HEAD_DIM).transpose(0, 2, 3, 1, 4)


def _diff_core(q, k, v, lam, scale):
    s = jnp.einsum('bhmqd,bhmkd->bhmqk', q, k).astype(jnp.float32) * scale
    p = jax.nn.softmax(s, axis=-1)
    a = (p[:, :, 0] - lam * p[:, :, 1]).astype(v.dtype)
    return jnp.einsum('bhqk,bhkv->bhqv', a, v)


def _diff_mixer(q, k, v, qc, kc, vc, lam_params, subln_g, lambda_init, cos, sin, need_ctx):
    scale = DIFF_HEAD_DIM ** -0.5
    lp = lam_params.astype(jnp.float32)
    lam = jnp.exp(jnp.sum(lp[0] * lp[1])) - jnp.exp(jnp.sum(lp[2] * lp[3])) + lambda_init
    q = _apply_axial_rope(_diff_heads(q), cos, sin)
    k = _apply_axial_rope(_diff_heads(k), cos, sin)
    v = _split_heads(v, DIFF_HEADS)
    kc = _diff_heads(kc)
    vc = _split_heads(vc, DIFF_HEADS)
    k_all = jnp.concatenate([k, kc], axis=3)
    v_all = jnp.concatenate([v, vc], axis=2)
    b, h, _, n, d = q.shape
    nb = n // Q_BLOCK
    q_blocks = jnp.moveaxis(q.reshape(b, h, 2, nb, Q_BLOCK, d), 3, 0)
    y = lax.map(lambda qb: _diff_core(qb, k_all, v_all, lam, scale), q_blocks)
    y = jnp.moveaxis(y, 0, 2).reshape(b, h, n, 2 * d)

    def finish(t):
        return _merge_heads(_rms_norm(t, subln_g) * (1.0 - lambda_init))

    y_ctx = None
    if need_ctx:
        y_ctx = finish(_diff_core(_diff_heads(qc), kc, vc, lam, scale))
    return finish(y), y_ctx


def setup_inputs(seed: int = 0) -> dict:
    key = jax.random.key(seed)
    ks = jax.random.split(key, 13)
    f32 = jnp.float32
    x = jax.random.normal(ks[0], (BATCH, SEQ, D_MODEL), f32)
    c = jax.random.normal(ks[1], (BATCH, D_MODEL), f32)
    ctx = jax.random.normal(ks[2], (BATCH, CTX_LEN, D_MODEL), f32)
    c_ctx = jax.random.normal(ks[3], (D_MODEL,), f32)
    w_mod = jax.random.normal(ks[4], (DEPTH, D_MODEL, 3 * D_MODEL), f32) * (0.5 * D_MODEL ** -0.5)
    b_mod = 0.02 * jax.random.normal(ks[5], (DEPTH, 3 * D_MODEL), f32)
    w_in = jax.random.normal(ks[6], (DEPTH, D_MODEL, 4 * WIDTH), f32) * (D_MODEL ** -0.5)
    w_out = jax.random.normal(ks[7], (DEPTH, WIDTH, D_MODEL), f32) * (WIDTH ** -0.5 * DEEPNORM_BETA)
    ln_g = 1.0 + 0.02 * jax.random.normal(ks[8], (DEPTH, D_MODEL), f32)
    ln_b = 0.02 * jax.random.normal(ks[9], (DEPTH, D_MODEL), f32)
    na_rpb = 0.1 * jax.random.normal(ks[10], (N_NA_LAYERS, NA_HEADS, 2 * NA_WIN_R - 1, 2 * NA_WIN_C - 1), f32)
    diff_lambda = 0.1 * jax.random.normal(ks[11], (N_DIFF_LAYERS, 4, DIFF_HEAD_DIM), f32)
    diff_subln_g = 1.0 + 0.02 * jax.random.normal(ks[12], (N_DIFF_LAYERS, 2 * DIFF_HEAD_DIM), f32)
    return {'x': x, 'c': c, 'ctx': ctx, 'c_ctx': c_ctx, 'w_mod': w_mod, 'b_mod': b_mod,
            'w_in': w_in, 'w_out': w_out, 'ln_g': ln_g, 'ln_b': ln_b, 'na_rpb': na_rpb,
            'diff_lambda': diff_lambda, 'diff_subln_g': diff_subln_g}


def reference(x, c, ctx, c_ctx, w_mod, b_mod, w_in, w_out, ln_g, ln_b, na_rpb, diff_lambda, diff_subln_g):
    cos, sin = _axial_rope_tables(x.shape[1])
    for l in range(DEPTH):
        need_ctx = l < DEPTH - 1
        sh, sc, g = _modulation(c, w_mod[l], b_mod[l])
        sh, sc, g = sh[:, None, :], sc[:, None, :], g[:, None, :]
        shc, scc, gc = _modulation(c_ctx, w_mod[l], b_mod[l])
        h = x * (1.0 + sc) + sh
        hc = ctx * (1.0 + scc) + shc
        q, k, v, z = jnp.split(h @ w_in[l], 4, axis=-1)
        if need_ctx:
            qc, kc, vc, zc = jnp.split(hc @ w_in[l], 4, axis=-1)
        else:
            kc, vc = jnp.split(hc @ w_in[l][:, WIDTH:3 * WIDTH], 2, axis=-1)
            qc, zc = None, None
        if l % N_MIXERS == 0:
            y, yc = _na_mixer(q, k, v, qc, kc, vc, na_rpb[l // N_MIXERS], need_ctx)
        else:
            lambda_init = 0.8 - 0.6 * math.exp(-0.3 * l)
            y, yc = _diff_mixer(q, k, v, qc, kc, vc, diff_lambda[l // N_MIXERS], diff_subln_g[l // N_MIXERS],
                                lambda_init, cos, sin, need_ctx)
        out = (y * jax.nn.silu(z)) @ w_out[l]
        x_new = _layer_norm(DEEPNORM_ALPHA * x + (1.0 + g) * out, ln_g[l], ln_b[l])
        if need_ctx:
            out_c = (yc * jax.nn.silu(zc)) @ w_out[l]
            ctx = _layer_norm(DEEPNORM_ALPHA * ctx + (1.0 + gc) * out_c, ln_g[l], ln_b[l])
        x = x_new
    return x
```

```python
import functools
import math

import jax
import jax.numpy as jnp
from jax import lax
from jax.experimental import pallas as pl
from jax.experimental.pallas import tpu as pltpu

F32 = jnp.float32
BF16 = jnp.bfloat16

GRID_W = 64
ROPE_BASE = 10000.0
LN_EPS = 1e-5
NEG = -1e30
LANES = 128
HEAD_DIM = 64
NA_BAND_ROWS = 8
NA_WIN_COLS = 16
VMEM_LIMIT = 48 * 1024 * 1024


def _cparams(sem):
    return pltpu.CompilerParams(dimension_semantics=sem, vmem_limit_bytes=VMEM_LIMIT)


def _mod_kernel(c_ref, w_ref, b_ref, o_ref):
    c = c_ref[...]
    s = c * jax.nn.sigmoid(c)
    o_ref[0] = jnp.dot(s, w_ref[0], preferred_element_type=F32) + b_ref[0]


def _modulation(cond, w_mod, b_mod):
    depth, d, d3 = w_mod.shape
    rows = cond.shape[0]
    tn = 1024
    return pl.pallas_call(
        _mod_kernel,
        grid=(depth, d3 // tn),
        in_specs=[
            pl.BlockSpec((rows, d), lambda l, j: (0, 0)),
            pl.BlockSpec((1, d, tn), lambda l, j: (l, 0, j)),
            pl.BlockSpec((1, 1, tn), lambda l, j: (l, 0, j)),
        ],
        out_specs=pl.BlockSpec((1, rows, tn), lambda l, j: (l, 0, j)),
        out_shape=jax.ShapeDtypeStruct((depth, rows, d3), F32),
        compiler_params=_cparams(("parallel", "parallel")),
        name="modulation",
    )(cond, w_mod, b_mod.reshape(depth, 1, d3))


def _rope_store(t, cos, sin, is_first, o_ref):
    for cidx in range(t.shape[1] // LANES):
        tc = t[:, cidx * LANES:(cidx + 1) * LANES]
        partner = jnp.where(is_first, pltpu.roll(tc, LANES - 16, axis=1), pltpu.roll(tc, 16, axis=1))
        o_ref[0, :, cidx * LANES:(cidx + 1) * LANES] = (tc * cos + partner * sin).astype(o_ref.dtype)


def _inproj_kernel(*refs, rope, v_transposed, with_qz, q_scale):
    it = iter(refs)
    x_ref, sc_ref, sh_ref = next(it), next(it), next(it)
    wq_ref = next(it) if with_qz else None
    wk_ref, wv_ref = next(it), next(it)
    wz_ref = next(it) if with_qz else None
    cos_ref, sin_ref = (next(it), next(it)) if rope else (None, None)
    q_ref = next(it) if with_qz else None
    k_ref, v_ref = next(it), next(it)
    z_ref = next(it) if with_qz else None

    h = (x_ref[0] * (1.0 + sc_ref[0]) + sh_ref[0]).astype(BF16)
    if rope:
        cos, sin = cos_ref[...], sin_ref[...]
        lane = lax.broadcasted_iota(jnp.int32, cos.shape, 1)
        is_first = (lane // 16) % 2 == 0
    if with_qz:
        q = jnp.dot(h, wq_ref[...], preferred_element_type=F32) * q_scale
        if rope:
            _rope_store(q, cos, sin, is_first, q_ref)
        else:
            q_ref[0] = q.astype(q_ref.dtype)
    k = jnp.dot(h, wk_ref[...], preferred_element_type=F32)
    if rope:
        _rope_store(k, cos, sin, is_first, k_ref)
    else:
        k_ref[0] = k.astype(k_ref.dtype)
    if v_transposed:
        v_ref[0] = lax.dot_general(wv_ref[...], h, (((1,), (1,)), ((), ())),
                                   preferred_element_type=F32).astype(v_ref.dtype)
    else:
        v_ref[0] = jnp.dot(h, wv_ref[...], preferred_element_type=F32).astype(v_ref.dtype)
    if with_qz:
        z_ref[0] = jnp.dot(h, wz_ref[...], preferred_element_type=F32).astype(z_ref.dtype)


def _inproj(x, sc, sh, wq, wk, wv, wz, *, tm, rope_tables, v_transposed, with_qz, q_scale):
    b, n, d = x.shape
    w = wk.shape[1]
    per_batch = sc.shape[0] == b
    mod_map = (lambda i, t: (i, 0, 0)) if per_batch else (lambda i, t: (0, 0, 0))
    const = lambda i, t: (0, 0)
    rope = rope_tables is not None
    wspec = pl.BlockSpec((d, w), const)
    row_spec = pl.BlockSpec((1, tm, w), lambda i, t: (i, t, 0))
    row_shape = jax.ShapeDtypeStruct((b, n, w), BF16)

    args = [x, sc, sh]
    in_specs = [pl.BlockSpec((1, tm, d), lambda i, t: (i, t, 0)),
                pl.BlockSpec((1, 1, d), mod_map), pl.BlockSpec((1, 1, d), mod_map)]
    out_shape, out_specs = [], []
    if with_qz:
        args.append(wq); in_specs.append(wspec)
        out_shape.append(row_shape); out_specs.append(row_spec)
    args += [wk, wv]
    in_specs += [wspec, pl.BlockSpec(wv.shape, const)]
    out_shape.append(row_shape); out_specs.append(row_spec)
    if v_transposed:
        out_shape.append(jax.ShapeDtypeStruct((b, w, n), BF16))
        out_specs.append(pl.BlockSpec((1, w, tm), lambda i, t: (i, 0, t)))
    else:
        out_shape.append(row_shape); out_specs.append(row_spec)
    if with_qz:
        args.append(wz); in_specs.append(wspec)
        out_shape.append(row_shape); out_specs.append(row_spec)
    if rope:
        args += list(rope_tables)
        in_specs += [pl.BlockSpec((tm, LANES), lambda i, t: (t, 0))] * 2

    outs = pl.pallas_call(
        functools.partial(_inproj_kernel, rope=rope, v_transposed=v_transposed,
                          with_qz=with_qz, q_scale=q_scale),
        grid=(b, n // tm),
        in_specs=in_specs,
        out_specs=out_specs,
        out_shape=out_shape,
        compiler_params=_cparams(("parallel", "parallel")),
        name="inproj",
    )(*args)
    if with_qz:
        return outs
    return None, outs[0], outs[1], None


def _stack_heads(q, first_head):
    zero = jnp.zeros_like(q)
    return jnp.concatenate([jnp.where(first_head, q, zero), jnp.where(first_head, zero, q)], axis=0)


_NT = (((1,), (1,)), ((), ()))


def _na_kernel(q_ref, k_ref, v_ref, kc_ref, vc_ref, bias_ref, o_ref, *, rows):
    kc = kc_ref[0]
    vc = vc_ref[0]
    first_head = lax.broadcasted_iota(jnp.int32, (GRID_W, LANES), 1) < HEAD_DIM
    band = NA_BAND_ROWS * GRID_W

    def body(r, carry):
        r0 = jnp.clip(r - NA_BAND_ROWS // 2, 0, rows - NA_BAND_ROWS)
        dr = r0 - r + NA_BAND_ROWS - 1
        q_rows = q_ref[0, pl.ds(pl.multiple_of(r * GRID_W, GRID_W), GRID_W), :]
        sq = _stack_heads(q_rows, first_head)
        start = pl.multiple_of(r0 * GRID_W, GRID_W)
        kb = k_ref[0, pl.ds(start, band), :]
        vb = v_ref[0, pl.ds(start, band), :]
        s_loc = lax.dot_general(sq, kb, _NT, preferred_element_type=F32)
        bias = jnp.concatenate([bias_ref[0, dr + 2 * p] for p in range(NA_BAND_ROWS // 2)], axis=1)
        s_loc = s_loc + bias
        s_ctx = lax.dot_general(sq, kc, _NT, preferred_element_type=F32)
        m = jnp.maximum(jnp.max(s_loc, axis=1, keepdims=True), jnp.max(s_ctx, axis=1, keepdims=True))
        p_loc = jnp.exp(s_loc - m)
        p_ctx = jnp.exp(s_ctx - m)
        denom = jnp.sum(p_loc, axis=1, keepdims=True) + jnp.sum(p_ctx, axis=1, keepdims=True)
        o2 = (jnp.dot(p_loc.astype(BF16), vb, preferred_element_type=F32)
              + jnp.dot(p_ctx.astype(BF16), vc, preferred_element_type=F32)) / denom
        y = jnp.where(first_head, o2[:GRID_W], o2[GRID_W:])
        o_ref[0, pl.ds(pl.multiple_of(r * GRID_W, GRID_W), GRID_W), :] = y.astype(o_ref.dtype)
        return carry

    lax.fori_loop(0, rows, body, 0)


def _na_attention(q, k, v, kc, vc, bias):
    b, n, w = q.shape
    c = kc.shape[1]
    rows = n // GRID_W
    hp = w // LANES
    tok = lambda i, j: (i, 0, j)
    return pl.pallas_call(
        functools.partial(_na_kernel, rows=rows),
        grid=(b, hp),
        in_specs=[
            pl.BlockSpec((1, n, LANES), tok), pl.BlockSpec((1, n, LANES), tok),
            pl.BlockSpec((1, n, LANES), tok),
            pl.BlockSpec((1, c, LANES), tok), pl.BlockSpec((1, c, LANES), tok),
            pl.BlockSpec((1,) + bias.shape[1:], lambda i, j: (j, 0, 0, 0)),
        ],
        out_specs=pl.BlockSpec((1, n, LANES), tok),
        out_shape=jax.ShapeDtypeStruct((b, n, w), BF16),
        compiler_params=_cparams(("parallel", "parallel")),
        name="na_attention",
    )(q, k, v, kc, vc, bias)


def _na_ctx_kernel(q_ref, kc_ref, vc_ref, o_ref):
    q = q_ref[0]
    n = q.shape[0]
    first_head = lax.broadcasted_iota(jnp.int32, q.shape, 1) < HEAD_DIM
    s = lax.dot_general(_stack_heads(q, first_head), kc_ref[0], _NT, preferred_element_type=F32)
    m = jnp.max(s, axis=1, keepdims=True)
    p = jnp.exp(s - m)
    o2 = jnp.dot(p.astype(BF16), vc_ref[0], preferred_element_type=F32) / jnp.sum(p, axis=1, keepdims=True)
    o_ref[0] = jnp.where(first_head, o2[:n], o2[n:]).astype(o_ref.dtype)


def _na_ctx_attention(qc, kc, vc):
    b, c, w = qc.shape
    tok = pl.BlockSpec((1, c, LANES), lambda i, j: (i, 0, j))
    return pl.pallas_call(
        _na_ctx_kernel,
        grid=(b, w // LANES),
        in_specs=[tok, tok, tok],
        out_specs=tok,
        out_shape=jax.ShapeDtypeStruct((b, c, w), BF16),
        compiler_params=_cparams(("parallel", "parallel")),
        name="na_ctx_attention",
    )(qc, kc, vc)


def _na_bias_table(rpb):
    h = rpb.shape[0]
    wq = jnp.arange(GRID_W)[:, None]
    kj = jnp.arange(GRID_W)[None, :]
    col_start = jnp.clip(wq - NA_WIN_COLS // 2, 0, GRID_W - NA_WIN_COLS)
    valid = (kj >= col_start) & (kj < col_start + NA_WIN_COLS)
    rel = jnp.clip(kj - wq + NA_WIN_COLS - 1, 0, 2 * NA_WIN_COLS - 2)
    t = jnp.where(valid[None, None], rpb[:, :, rel], NEG)
    t = t.reshape(h // 2, 2, 2 * NA_BAND_ROWS - 1, GRID_W, GRID_W)
    t = jnp.concatenate([t[:, 0], t[:, 1]], axis=2)
    return jnp.concatenate([t[:, :-1], t[:, 1:]], axis=3).astype(F32)


def _diff_kernel(*refs, n_lat, kb, lambda_init):
    it = iter(refs)
    q_ref = next(it)
    k_ref, vt_ref = (next(it), next(it)) if n_lat else (None, None)
    kc_ref, vct_ref, lam_ref, g_ref, o_ref, acc_ref = (next(it) for _ in range(6))

    q = q_ref[0]
    tq = q.shape[0]
    first_map = lax.broadcasted_iota(jnp.int32, q.shape, 1) < HEAD_DIM
    zero = jnp.zeros_like(q)
    qm = (jnp.where(first_map, q, zero), jnp.where(first_map, zero, q))
    acc_ref[...] = jnp.zeros_like(acc_ref)

    def step(kblk, vtblk, stats):
        new = []
        for mp in range(2):
            m_old, l_old = stats[mp]
            st = lax.dot_general(kblk, qm[mp], _NT, preferred_element_type=F32)
            m_new = jnp.maximum(m_old, jnp.max(st, axis=0, keepdims=True))
            alpha = jnp.exp(m_old - m_new)
            p = jnp.exp(st - m_new)
            l_new = alpha * l_old + jnp.sum(p, axis=0, keepdims=True)
            acc_ref[mp] = alpha * acc_ref[mp] + jnp.dot(vtblk, p.astype(BF16),
                                                        preferred_element_type=F32)
            new.append((m_new, l_new))
        return tuple(new)

    init = tuple((jnp.full((1, tq), NEG, F32), jnp.zeros((1, tq), F32)) for _ in range(2))
    stats = init
    if n_lat:
        def body(j, st):
            off = pl.multiple_of(j * kb, kb)
            return step(k_ref[0, pl.ds(off, kb), :], vt_ref[0, :, pl.ds(off, kb)], st)
        stats = lax.fori_loop(0, n_lat // kb, body, stats)
    stats = step(kc_ref[0], vct_ref[0], stats)

    lp = lam_ref[...]
    lam = (jnp.exp(jnp.sum(lp[0:1] * lp[1:2], axis=1, keepdims=True))
           - jnp.exp(jnp.sum(lp[2:3] * lp[3:4], axis=1, keepdims=True)) + lambda_init)
    ot = acc_ref[0] / stats[0][1] - lam * (acc_ref[1] / stats[1][1])
    ms = jnp.mean(ot * ot, axis=0, keepdims=True)
    ot = ot * lax.rsqrt(ms + LN_EPS)
    o_ref[0] = (ot.T * g_ref[...] * (1.0 - lambda_init)).astype(o_ref.dtype)


def _diff_attention(q, k, vt, kc, vct, lam_params, subln_g, *, lambda_init, tq, kb):
    b, nq, w = q.shape
    c = kc.shape[1]
    heads = w // LANES
    n_lat = 0 if k is None else k.shape[1]
    args = [q]
    in_specs = [pl.BlockSpec((1, tq, LANES), lambda i, h, t: (i, t, h))]
    if n_lat:
        args += [k, vt]
        in_specs += [pl.BlockSpec((1, n_lat, LANES), lambda i, h, t: (i, 0, h)),
                     pl.BlockSpec((1, LANES, n_lat), lambda i, h, t: (i, h, 0))]
    args += [kc, vct, lam_params, subln_g.reshape(1, LANES)]
    in_specs += [pl.BlockSpec((1, c, LANES), lambda i, h, t: (i, 0, h)),
                 pl.BlockSpec((1, LANES, c), lambda i, h, t: (i, h, 0)),
                 pl.BlockSpec(lam_params.shape, lambda i, h, t: (0, 0)),
                 pl.BlockSpec((1, LANES), lambda i, h, t: (0, 0))]
    return pl.pallas_call(
        functools.partial(_diff_kernel, n_lat=n_lat, kb=kb, lambda_init=lambda_init),
        grid=(b, heads, nq // tq),
        in_specs=in_specs,
        out_specs=pl.BlockSpec((1, tq, LANES), lambda i, h, t: (i, t, h)),
        out_shape=jax.ShapeDtypeStruct((b, nq, w), BF16),
        scratch_shapes=[pltpu.VMEM((2, LANES, tq), F32)],
        compiler_params=_cparams(("parallel", "parallel", "arbitrary")),
        name="diff_attention",
    )(*args)


def _outproj_kernel(y_ref, z_ref, x_ref, g_ref, w_ref, lng_ref, lnb_ref, o_ref, *, alpha):
    z = z_ref[0].astype(F32)
    u = (y_ref[0].astype(F32) * (z * jax.nn.sigmoid(z))).astype(BF16)
    out = jnp.dot(u, w_ref[...], preferred_element_type=F32)
    r = alpha * x_ref[0] + (1.0 + g_ref[0]) * out
    mu = jnp.mean(r, axis=1, keepdims=True)
    rc = r - mu
    var = jnp.mean(rc * rc, axis=1, keepdims=True)
    o_ref[0] = rc * lax.rsqrt(var + LN_EPS) * lng_ref[...] + lnb_ref[...]


def _outproj(y, z, x, gate, w_out, ln_g, ln_b, *, tm, alpha):
    b, n, d = x.shape
    w = y.shape[2]
    per_batch = gate.shape[0] == b
    mod_map = (lambda i, t: (i, 0, 0)) if per_batch else (lambda i, t: (0, 0, 0))
    row = lambda i, t: (i, t, 0)
    const = lambda i, t: (0, 0)
    return pl.pallas_call(
        functools.partial(_outproj_kernel, alpha=alpha),
        grid=(b, n // tm),
        in_specs=[
            pl.BlockSpec((1, tm, w), row), pl.BlockSpec((1, tm, w), row),
            pl.BlockSpec((1, tm, d), row), pl.BlockSpec((1, 1, d), mod_map),
            pl.BlockSpec((w, d), const), pl.BlockSpec((1, d), const), pl.BlockSpec((1, d), const),
        ],
        out_specs=pl.BlockSpec((1, tm, d), row),
        out_shape=jax.ShapeDtypeStruct((b, n, d), F32),
        compiler_params=_cparams(("parallel", "parallel")),
        name="outproj_ln",
    )(y, z, x, gate, w_out, ln_g.reshape(1, d), ln_b.reshape(1, d))


def _rope_tables(n_tok):
    t = jnp.arange(n_tok, dtype=jnp.int32)
    row = (t // GRID_W).astype(F32)[:, None]
    col = (t % GRID_W).astype(F32)[:, None]
    n_freq = HEAD_DIM // 4
    inv_freq = ROPE_BASE ** (-jnp.arange(n_freq, dtype=F32) / n_freq)
    lane = jnp.arange(LANES)
    pos = jnp.where(((lane % HEAD_DIM) // (HEAD_DIM // 2) == 0)[None, :], row, col)
    ang = pos * inv_freq[lane % n_freq][None, :]
    first = ((lane // n_freq) % 2 == 0)[None, :]
    return jnp.cos(ang), jnp.where(first, -jnp.sin(ang), jnp.sin(ang))


def kernel(x, c, ctx, c_ctx, w_mod, b_mod, w_in, w_out, ln_g, ln_b, na_rpb, diff_lambda, diff_subln_g):
    depth, d, _ = w_mod.shape
    b, n, _ = x.shape
    w = w_out.shape[1]
    n_ctx = ctx.shape[1]
    alpha = (2.0 * depth) ** 0.25
    q_scale = HEAD_DIM ** -0.5

    cond = jnp.concatenate([c, c_ctx[None, :], jnp.zeros((16 - b - 1, d), F32)], axis=0)
    mods = _modulation(cond, w_mod, b_mod)
    w_in_bf = w_in.astype(BF16)
    w_out_bf = w_out.astype(BF16)
    rope_tables = _rope_tables(n)

    for l in range(depth):
        need_ctx = l < depth - 1
        is_diff = l % 2 == 1
        m = mods[l]
        sh, sc, g = (m[:b, None, i * d:(i + 1) * d] for i in range(3))
        shc, scc, gc = (m[b:b + 1, None, i * d:(i + 1) * d] for i in range(3))
        wq, wk, wv, wz = (w_in_bf[l, :, i * w:(i + 1) * w] for i in range(4))
        if is_diff:
            wv = wv.T
        q, k, v, z = _inproj(x, sc, sh, wq, wk, wv, wz, tm=512,
                             rope_tables=rope_tables if is_diff else None,
                             v_transposed=is_diff, with_qz=True, q_scale=q_scale)
        qc, kc, vc, zc = _inproj(ctx, scc, shc, wq, wk, wv, wz, tm=n_ctx, rope_tables=None,
                                 v_transposed=is_diff, with_qz=need_ctx, q_scale=q_scale)
        if is_diff:
            lam_p, sub_g = diff_lambda[l // 2], diff_subln_g[l // 2]
            lambda_init = 0.8 - 0.6 * math.exp(-0.3 * l)
            y = _diff_attention(q, k, v, kc, vc, lam_p, sub_g, lambda_init=lambda_init, tq=256, kb=512)
            if need_ctx:
                yc = _diff_attention(qc, None, None, kc, vc, lam_p, sub_g,
                                     lambda_init=lambda_init, tq=n_ctx, kb=512)
        else:
            y = _na_attention(q, k, v, kc, vc, _na_bias_table(na_rpb[l // 2]))
            if need_ctx:
                yc = _na_ctx_attention(qc, kc, vc)
        x_new = _outproj(y, z, x, g, w_out_bf[l], ln_g[l], ln_b[l], tm=512, alpha=alpha)
        if need_ctx:
            ctx = _outproj(yc, zc, ctx, gc, w_out_bf[l], ln_g[l], ln_b[l], tm=n_ctx, alpha=alpha)
        x = x_new
    return x
```

```python
import functools
import math

import jax
import jax.numpy as jnp
from jax import lax
from jax.experimental import pallas as pl
from jax.experimental.pallas import tpu as pltpu

F32 = jnp.float32
BF16 = jnp.bfloat16

GRID_W = 64
ROPE_BASE = 10000.0
LN_EPS = 1e-5
NEG = -1e30
LANES = 128
HEAD_DIM = 64
NA_BAND_ROWS = 8
NA_WIN_COLS = 16
NA_GROUP_ROWS = 2
SUBLANES_BF16 = 16
LOG2E = math.log2(math.e)
VMEM_LIMIT = 48 * 1024 * 1024


def _cparams(sem):
    return pltpu.CompilerParams(dimension_semantics=sem, vmem_limit_bytes=VMEM_LIMIT)


def _mod_kernel(c_ref, w_ref, b_ref, o_ref):
    c = c_ref[...]
    s = c * jax.nn.sigmoid(c)
    o_ref[0] = jnp.dot(s, w_ref[0], preferred_element_type=F32) + b_ref[0]


def _modulation(cond, w_mod, b_mod):
    depth, d, d3 = w_mod.shape
    rows = cond.shape[0]
    tn = 1024
    return pl.pallas_call(
        _mod_kernel,
        grid=(depth, d3 // tn),
        in_specs=[
            pl.BlockSpec((rows, d), lambda l, j: (0, 0)),
            pl.BlockSpec((1, d, tn), lambda l, j: (l, 0, j)),
            pl.BlockSpec((1, 1, tn), lambda l, j: (l, 0, j)),
        ],
        out_specs=pl.BlockSpec((1, rows, tn), lambda l, j: (l, 0, j)),
        out_shape=jax.ShapeDtypeStruct((depth, rows, d3), F32),
        compiler_params=_cparams(("parallel", "parallel")),
        name="modulation",
    )(cond, w_mod, b_mod.reshape(depth, 1, d3))


def _rope_store(t, cos, sin, is_first, o_ref):
    for cidx in range(t.shape[1] // LANES):
        tc = t[:, cidx * LANES:(cidx + 1) * LANES]
        partner = jnp.where(is_first, pltpu.roll(tc, LANES - 16, axis=1), pltpu.roll(tc, 16, axis=1))
        o_ref[0, :, cidx * LANES:(cidx + 1) * LANES] = (tc * cos + partner * sin).astype(o_ref.dtype)


def _inproj_kernel(*refs, rope, v_transposed, with_qz, q_scale):
    it = iter(refs)
    x_ref, sc_ref, sh_ref = next(it), next(it), next(it)
    wq_ref = next(it) if with_qz else None
    wk_ref, wv_ref = next(it), next(it)
    wz_ref = next(it) if with_qz else None
    cos_ref, sin_ref = (next(it), next(it)) if rope else (None, None)
    q_ref = next(it) if with_qz else None
    k_ref, v_ref = next(it), next(it)
    z_ref = next(it) if with_qz else None

    h = (x_ref[0] * (1.0 + sc_ref[0]) + sh_ref[0]).astype(BF16)
    if rope:
        cos, sin = cos_ref[...], sin_ref[...]
        lane = lax.broadcasted_iota(jnp.int32, cos.shape, 1)
        is_first = (lane // 16) % 2 == 0
    if with_qz:
        q = jnp.dot(h, wq_ref[...], preferred_element_type=F32) * q_scale
        if rope:
            _rope_store(q, cos, sin, is_first, q_ref)
        else:
            q_ref[0] = q.astype(q_ref.dtype)
    k = jnp.dot(h, wk_ref[...], preferred_element_type=F32)
    if rope:
        _rope_store(k, cos, sin, is_first, k_ref)
    else:
        k_ref[0] = k.astype(k_ref.dtype)
    if v_transposed:
        v_ref[0] = lax.dot_general(wv_ref[...], h, (((1,), (1,)), ((), ())),
                                   preferred_element_type=F32).astype(v_ref.dtype)
    else:
        v_ref[0] = jnp.dot(h, wv_ref[...], preferred_element_type=F32).astype(v_ref.dtype)
    if with_qz:
        z_ref[0] = jnp.dot(h, wz_ref[...], preferred_element_type=F32).astype(z_ref.dtype)


def _inproj(x, sc, sh, wq, wk, wv, wz, *, tm, rope_tables, v_transposed, with_qz, q_scale):
    b, n, d = x.shape
    w = wk.shape[1]
    per_batch = sc.shape[0] == b
    mod_map = (lambda i, t: (i, 0, 0)) if per_batch else (lambda i, t: (0, 0, 0))
    const = lambda i, t: (0, 0)
    rope = rope_tables is not None
    wspec = pl.BlockSpec((d, w), const)
    row_spec = pl.BlockSpec((1, tm, w), lambda i, t: (i, t, 0))
    row_shape = jax.ShapeDtypeStruct((b, n, w), BF16)

    args = [x, sc, sh]
    in_specs = [pl.BlockSpec((1, tm, d), lambda i, t: (i, t, 0)),
                pl.BlockSpec((1, 1, d), mod_map), pl.BlockSpec((1, 1, d), mod_map)]
    out_shape, out_specs = [], []
    if with_qz:
        args.append(wq); in_specs.append(wspec)
        out_shape.append(row_shape); out_specs.append(row_spec)
    args += [wk, wv]
    in_specs += [wspec, pl.BlockSpec(wv.shape, const)]
    out_shape.append(row_shape); out_specs.append(row_spec)
    if v_transposed:
        out_shape.append(jax.ShapeDtypeStruct((b, w, n), BF16))
        out_specs.append(pl.BlockSpec((1, w, tm), lambda i, t: (i, 0, t)))
    else:
        out_shape.append(row_shape); out_specs.append(row_spec)
    if with_qz:
        args.append(wz); in_specs.append(wspec)
        out_shape.append(row_shape); out_specs.append(row_spec)
    if rope:
        args += list(rope_tables)
        in_specs += [pl.BlockSpec((tm, LANES), lambda i, t: (t, 0))] * 2

    outs = pl.pallas_call(
        functools.partial(_inproj_kernel, rope=rope, v_transposed=v_transposed,
                          with_qz=with_qz, q_scale=q_scale),
        grid=(b, n // tm),
        in_specs=in_specs,
        out_specs=out_specs,
        out_shape=out_shape,
        compiler_params=_cparams(("parallel", "parallel")),
        name="inproj",
    )(*args)
    if with_qz:
        return outs
    return None, outs[0], outs[1], None


def _stack_heads(q, first_head):
    zero = jnp.zeros_like(q)
    return jnp.concatenate([jnp.where(first_head, q, zero), jnp.where(first_head, zero, q)], axis=0)


_NT = (((1,), (1,)), ((), ()))


def _na_kernel(q_ref, k_ref, v_ref, kc_ref, vc_ref, bias_ref, o_ref, s_ref, *, rows):
    kc = kc_ref[0]
    vc = vc_ref[0]
    first_head = lax.broadcasted_iota(jnp.int32, (GRID_W, LANES), 1) < HEAD_DIM
    band = NA_BAND_ROWS * GRID_W
    group = s_ref.shape[1] // (2 * GRID_W)
    n_groups = rows // group
    assert rows % group == 0 and n_groups % 2 == 0 and n_groups >= 4

    def band_start(r):
        r0 = jnp.clip(r - NA_BAND_ROWS // 2, 0, rows - NA_BAND_ROWS)
        return r0, pl.multiple_of(r0 * GRID_W, GRID_W)

    def stage_a(g, slot):
        for t in range(group):
            r = g * group + t
            r0, start = band_start(r)
            dr = r0 - r + NA_BAND_ROWS - 1
            q_rows = q_ref[0, pl.ds(pl.multiple_of(r * GRID_W, GRID_W), GRID_W), :]
            sq = _stack_heads(q_rows, first_head)
            s_loc = lax.dot_general(sq, k_ref[0, pl.ds(start, band), :], _NT, preferred_element_type=F32)
            bias = jnp.concatenate([bias_ref[0, dr + 2 * p] for p in range(NA_BAND_ROWS // 2)], axis=1)
            s_loc = s_loc + bias
            s_ctx = lax.dot_general(sq, kc, _NT, preferred_element_type=F32)
            m = jnp.maximum(jnp.max(s_loc, axis=1, keepdims=True), jnp.max(s_ctx, axis=1, keepdims=True))
            lo = t * 2 * GRID_W
            s_ref[slot, lo:lo + 2 * GRID_W, 0:band] = s_loc - m
            s_ref[slot, lo:lo + 2 * GRID_W, band:] = s_ctx - m

    def stage_b(g, slot):
        for t in range(group):
            r = g * group + t
            _, start = band_start(r)
            lo = t * 2 * GRID_W
            p = jnp.exp2(s_ref[slot, lo:lo + 2 * GRID_W, :])
            denom = jnp.sum(p, axis=1, keepdims=True)
            pb = p.astype(BF16)
            o2 = (jnp.dot(pb[:, :band], v_ref[0, pl.ds(start, band), :], preferred_element_type=F32)
                  + jnp.dot(pb[:, band:], vc, preferred_element_type=F32)) / denom
            y = jnp.where(first_head, o2[:GRID_W], o2[GRID_W:])
            o_ref[0, pl.ds(pl.multiple_of(r * GRID_W, GRID_W), GRID_W), :] = y.astype(o_ref.dtype)

    stage_a(0, 0)

    def body(i, carry):
        stage_a(2 * i + 1, 1)
        stage_b(2 * i, 0)
        stage_a(2 * i + 2, 0)
        stage_b(2 * i + 1, 1)
        return carry

    lax.fori_loop(0, n_groups // 2 - 1, body, 0)
    stage_a(n_groups - 1, 1)
    stage_b(n_groups - 2, 0)
    stage_b(n_groups - 1, 1)


def _na_attention(q, k, v, kc, vc, bias):
    b, n, w = q.shape
    c = kc.shape[1]
    rows = n // GRID_W
    hp = w // LANES
    tok = lambda i, j: (i, 0, j)
    return pl.pallas_call(
        functools.partial(_na_kernel, rows=rows),
        grid=(b, hp),
        in_specs=[
            pl.BlockSpec((1, n, LANES), tok), pl.BlockSpec((1, n, LANES), tok),
            pl.BlockSpec((1, n, LANES), tok),
            pl.BlockSpec((1, c, LANES), tok), pl.BlockSpec((1, c, LANES), tok),
            pl.BlockSpec((1,) + bias.shape[1:], lambda i, j: (j, 0, 0, 0)),
        ],
        out_specs=pl.BlockSpec((1, n, LANES), tok),
        out_shape=jax.ShapeDtypeStruct((b, n, w), BF16),
        scratch_shapes=[pltpu.VMEM((2, NA_GROUP_ROWS * 2 * GRID_W, NA_BAND_ROWS * GRID_W + c), F32)],
        compiler_params=_cparams(("parallel", "parallel")),
        name="na_attention",
    )(q, k, v, kc, vc, bias)


def _na_ctx_kernel(q_ref, kc_ref, vc_ref, o_ref):
    q = q_ref[0]
    n = q.shape[0]
    first_head = lax.broadcasted_iota(jnp.int32, q.shape, 1) < HEAD_DIM
    s = lax.dot_general(_stack_heads(q, first_head), kc_ref[0], _NT, preferred_element_type=F32)
    m = jnp.max(s, axis=1, keepdims=True)
    p = jnp.exp2(s - m)
    o2 = jnp.dot(p.astype(BF16), vc_ref[0], preferred_element_type=F32) / jnp.sum(p, axis=1, keepdims=True)
    o_ref[0] = jnp.where(first_head, o2[:n], o2[n:]).astype(o_ref.dtype)


def _na_ctx_attention(qc, kc, vc):
    b, c, w = qc.shape
    tok = pl.BlockSpec((1, c, LANES), lambda i, j: (i, 0, j))
    return pl.pallas_call(
        _na_ctx_kernel,
        grid=(b, w // LANES),
        in_specs=[tok, tok, tok],
        out_specs=tok,
        out_shape=jax.ShapeDtypeStruct((b, c, w), BF16),
        compiler_params=_cparams(("parallel", "parallel")),
        name="na_ctx_attention",
    )(qc, kc, vc)


def _na_bias_table(rpb):
    h = rpb.shape[0]
    wq = jnp.arange(GRID_W)[:, None]
    kj = jnp.arange(GRID_W)[None, :]
    col_start = jnp.clip(wq - NA_WIN_COLS // 2, 0, GRID_W - NA_WIN_COLS)
    valid = (kj >= col_start) & (kj < col_start + NA_WIN_COLS)
    rel = jnp.clip(kj - wq + NA_WIN_COLS - 1, 0, 2 * NA_WIN_COLS - 2)
    t = jnp.where(valid[None, None], rpb[:, :, rel], NEG)
    t = t.reshape(h // 2, 2, 2 * NA_BAND_ROWS - 1, GRID_W, GRID_W)
    t = jnp.concatenate([t[:, 0], t[:, 1]], axis=2)
    return jnp.concatenate([t[:, :-1], t[:, 1:]], axis=3).astype(F32)


def _diff_kernel(*refs, n_lat, kb, lambda_init):
    it = iter(refs)
    q_ref = next(it)
    k_ref, vt_ref = (next(it), next(it)) if n_lat else (None, None)
    kc_ref, vct_ref, lam_ref, g_ref, o_ref, acc_ref, s_ref = (next(it) for _ in range(7))

    q = q_ref[0]
    tq = q.shape[0]
    n_ctx = kc_ref.shape[1]
    first_map = lax.broadcasted_iota(jnp.int32, q.shape, 1) < HEAD_DIM
    qcat = _stack_heads(q, first_map)
    acc_ref[...] = jnp.zeros_like(acc_ref)

    def scores(kblk, slot):
        st = lax.dot_general(kblk, qcat, _NT, preferred_element_type=F32)
        s_ref[slot, 0:kblk.shape[0], :] = st
        return jnp.max(st, axis=0, keepdims=True)

    def consume(slot, vtblk, bmax, m_old):
        m_new = jnp.maximum(m_old, bmax)
        alpha = jnp.exp2(m_old - m_new)
        p = jnp.exp2(s_ref[slot, 0:vtblk.shape[1], :] - m_new)
        vaug = jnp.concatenate([vtblk, jnp.ones((SUBLANES_BF16, vtblk.shape[1]), BF16)], axis=0)
        acc_ref[...] = alpha * acc_ref[...] + jnp.dot(vaug, p.astype(BF16), preferred_element_type=F32)
        return m_new

    def kblock(j):
        return k_ref[0, pl.ds(pl.multiple_of(j * kb, kb), kb), :]

    def vblock(j):
        return vt_ref[0, :, pl.ds(pl.multiple_of(j * kb, kb), kb)]

    m = jnp.full((1, 2 * tq), NEG, F32)
    if n_lat:
        nb = n_lat // kb
        assert nb >= 2 and nb % 2 == 0 and n_ctx <= kb
        bm_a = scores(kblock(0), 0)

        def body(i, carry):
            m, bm_a = carry
            bm_b = scores(kblock(2 * i + 1), 1)
            m = consume(0, vblock(2 * i), bm_a, m)
            bm_a = scores(kblock(2 * i + 2), 0)
            m = consume(1, vblock(2 * i + 1), bm_b, m)
            return m, bm_a

        m, bm_a = lax.fori_loop(0, nb // 2 - 1, body, (m, bm_a))
        bm_b = scores(kblock(nb - 1), 1)
        m = consume(0, vblock(nb - 2), bm_a, m)
        bm_a = scores(kc_ref[0], 0)
        m = consume(1, vblock(nb - 1), bm_b, m)
    else:
        bm_a = scores(kc_ref[0], 0)
    consume(0, vct_ref[0], bm_a, m)

    lp = lam_ref[...]
    lam = (jnp.exp(jnp.sum(lp[0:1] * lp[1:2], axis=1, keepdims=True))
           - jnp.exp(jnp.sum(lp[2:3] * lp[3:4], axis=1, keepdims=True)) + lambda_init)
    o_all = acc_ref[0:LANES, :] / acc_ref[LANES:LANES + 1, :]
    ot = o_all[:, :tq] - lam * o_all[:, tq:]
    ms = jnp.mean(ot * ot, axis=0, keepdims=True)
    ot = ot * lax.rsqrt(ms + LN_EPS)
    o_ref[0] = (ot.T * g_ref[...] * (1.0 - lambda_init)).astype(o_ref.dtype)


def _diff_attention(q, k, vt, kc, vct, lam_params, subln_g, *, lambda_init, tq, kb):
    b, nq, w = q.shape
    c = kc.shape[1]
    heads = w // LANES
    n_lat = 0 if k is None else k.shape[1]
    args = [q]
    in_specs = [pl.BlockSpec((1, tq, LANES), lambda i, h, t: (i, t, h))]
    if n_lat:
        args += [k, vt]
        in_specs += [pl.BlockSpec((1, n_lat, LANES), lambda i, h, t: (i, 0, h)),
                     pl.BlockSpec((1, LANES, n_lat), lambda i, h, t: (i, h, 0))]
    args += [kc, vct, lam_params, subln_g.reshape(1, LANES)]
    in_specs += [pl.BlockSpec((1, c, LANES), lambda i, h, t: (i, 0, h)),
                 pl.BlockSpec((1, LANES, c), lambda i, h, t: (i, h, 0)),
                 pl.BlockSpec(lam_params.shape, lambda i, h, t: (0, 0)),
                 pl.BlockSpec((1, LANES), lambda i, h, t: (0, 0))]
    return pl.pallas_call(
        functools.partial(_diff_kernel, n_lat=n_lat, kb=kb, lambda_init=lambda_init),
        grid=(b, heads, nq // tq),
        in_specs=in_specs,
        out_specs=pl.BlockSpec((1, tq, LANES), lambda i, h, t: (i, t, h)),
        out_shape=jax.ShapeDtypeStruct((b, nq, w), BF16),
        scratch_shapes=[pltpu.VMEM((LANES + SUBLANES_BF16, 2 * tq), F32),
                        pltpu.VMEM((2, kb, 2 * tq), F32)],
        compiler_params=_cparams(("parallel", "parallel", "arbitrary")),
        name="diff_attention",
    )(*args)


def _outproj_kernel(y_ref, z_ref, x_ref, g_ref, w_ref, lng_ref, lnb_ref, o_ref, *, alpha):
    z = z_ref[0].astype(F32)
    u = (y_ref[0].astype(F32) * (z * jax.nn.sigmoid(z))).astype(BF16)
    out = jnp.dot(u, w_ref[...], preferred_element_type=F32)
    r = alpha * x_ref[0] + (1.0 + g_ref[0]) * out
    mu = jnp.mean(r, axis=1, keepdims=True)
    rc = r - mu
    var = jnp.mean(rc * rc, axis=1, keepdims=True)
    o_ref[0] = rc * lax.rsqrt(var + LN_EPS) * lng_ref[...] + lnb_ref[...]


def _outproj(y, z, x, gate, w_out, ln_g, ln_b, *, tm, alpha):
    b, n, d = x.shape
    w = y.shape[2]
    per_batch = gate.shape[0] == b
    mod_map = (lambda i, t: (i, 0, 0)) if per_batch else (lambda i, t: (0, 0, 0))
    row = lambda i, t: (i, t, 0)
    const = lambda i, t: (0, 0)
    return pl.pallas_call(
        functools.partial(_outproj_kernel, alpha=alpha),
        grid=(b, n // tm),
        in_specs=[
            pl.BlockSpec((1, tm, w), row), pl.BlockSpec((1, tm, w), row),
            pl.BlockSpec((1, tm, d), row), pl.BlockSpec((1, 1, d), mod_map),
            pl.BlockSpec((w, d), const), pl.BlockSpec((1, d), const), pl.BlockSpec((1, d), const),
        ],
        out_specs=pl.BlockSpec((1, tm, d), row),
        out_shape=jax.ShapeDtypeStruct((b, n, d), F32),
        compiler_params=_cparams(("parallel", "parallel")),
        name="outproj_ln",
    )(y, z, x, gate, w_out, ln_g.reshape(1, d), ln_b.reshape(1, d))


def _rope_tables(n_tok):
    t = jnp.arange(n_tok, dtype=jnp.int32)
    row = (t // GRID_W).astype(F32)[:, None]
    col = (t % GRID_W).astype(F32)[:, None]
    n_freq = HEAD_DIM // 4
    inv_freq = ROPE_BASE ** (-jnp.arange(n_freq, dtype=F32) / n_freq)
    lane = jnp.arange(LANES)
    pos = jnp.where(((lane % HEAD_DIM) // (HEAD_DIM // 2) == 0)[None, :], row, col)
    ang = pos * inv_freq[lane % n_freq][None, :]
    first = ((lane // n_freq) % 2 == 0)[None, :]
    return jnp.cos(ang), jnp.where(first, -jnp.sin(ang), jnp.sin(ang))


def kernel(x, c, ctx, c_ctx, w_mod, b_mod, w_in, w_out, ln_g, ln_b, na_rpb, diff_lambda, diff_subln_g):
    depth, d, _ = w_mod.shape
    b, n, _ = x.shape
    w = w_out.shape[1]
    n_ctx = ctx.shape[1]
    alpha = (2.0 * depth) ** 0.25
    q_scale = HEAD_DIM ** -0.5 * LOG2E

    cond = jnp.concatenate([c, c_ctx[None, :], jnp.zeros((16 - b - 1, d), F32)], axis=0)
    mods = _modulation(cond, w_mod, b_mod)
    w_in_bf = w_in.astype(BF16)
    w_out_bf = w_out.astype(BF16)
    rope_tables = _rope_tables(n)

    for l in range(depth):
        need_ctx = l < depth - 1
        is_diff = l % 2 == 1
        m = mods[l]
        sh, sc, g = (m[:b, None, i * d:(i + 1) * d] for i in range(3))
        shc, scc, gc = (m[b:b + 1, None, i * d:(i + 1) * d] for i in range(3))
        wq, wk, wv, wz = (w_in_bf[l, :, i * w:(i + 1) * w] for i in range(4))
        if is_diff:
            wv = wv.T
        q, k, v, z = _inproj(x, sc, sh, wq, wk, wv, wz, tm=512,
                             rope_tables=rope_tables if is_diff else None,
                             v_transposed=is_diff, with_qz=True, q_scale=q_scale)
        qc, kc, vc, zc = _inproj(ctx, scc, shc, wq, wk, wv, wz, tm=n_ctx, rope_tables=None,
                                 v_transposed=is_diff, with_qz=need_ctx, q_scale=q_scale)
        if is_diff:
            lam_p, sub_g = diff_lambda[l // 2], diff_subln_g[l // 2]
            lambda_init = 0.8 - 0.6 * math.exp(-0.3 * l)
            y = _diff_attention(q, k, v, kc, vc, lam_p, sub_g, lambda_init=lambda_init, tq=256, kb=512)
            if need_ctx:
                yc = _diff_attention(qc, None, None, kc, vc, lam_p, sub_g,
                                     lambda_init=lambda_init, tq=n_ctx, kb=512)
        else:
            y = _na_attention(q, k, v, kc, vc, _na_bias_table(na_rpb[l // 2] * LOG2E))
            if need_ctx:
                yc = _na_ctx_attention(qc, kc, vc)
        x_new = _outproj(y, z, x, g, w_out_bf[l], ln_g[l], ln_b[l], tm=512, alpha=alpha)
        if need_ctx:
            ctx = _outproj(yc, zc, ctx, gc, w_out_bf[l], ln_g[l], ln_b[l], tm=n_ctx, alpha=alpha)
        x = x_new
    return x
```

```python
import functools
import math

import jax
import jax.numpy as jnp
from jax import lax
from jax.experimental import pallas as pl
from jax.experimental.pallas import tpu as pltpu

F32 = jnp.float32
BF16 = jnp.bfloat16

GRID_W = 64
ROPE_BASE = 10000.0
LN_EPS = 1e-5
NEG = -1e30
LANES = 128
HEAD_DIM = 64
NA_BAND_ROWS = 8
NA_WIN_COLS = 16
NA_GROUP_ROWS = 2
SUBLANES_BF16 = 16
LOG2E = math.log2(math.e)
VMEM_LIMIT = 48 * 1024 * 1024


def _cparams(sem):
    return pltpu.CompilerParams(dimension_semantics=sem, vmem_limit_bytes=VMEM_LIMIT)


def _mod_kernel(c_ref, w_ref, b_ref, o_ref):
    c = c_ref[...]
    s = c * jax.nn.sigmoid(c)
    o_ref[0] = jnp.dot(s, w_ref[0], preferred_element_type=F32) + b_ref[0]


def _modulation(cond, w_mod, b_mod):
    depth, d, d3 = w_mod.shape
    rows = cond.shape[0]
    tn = 1024
    return pl.pallas_call(
        _mod_kernel,
        grid=(depth, d3 // tn),
        in_specs=[
            pl.BlockSpec((rows, d), lambda l, j: (0, 0)),
            pl.BlockSpec((1, d, tn), lambda l, j: (l, 0, j)),
            pl.BlockSpec((1, 1, tn), lambda l, j: (l, 0, j)),
        ],
        out_specs=pl.BlockSpec((1, rows, tn), lambda l, j: (l, 0, j)),
        out_shape=jax.ShapeDtypeStruct((depth, rows, d3), F32),
        compiler_params=_cparams(("parallel", "parallel")),
        name="modulation",
    )(cond, w_mod, b_mod.reshape(depth, 1, d3))


def _rope_store(t, cos, sin, is_first, o_ref):
    for cidx in range(t.shape[1] // LANES):
        tc = t[:, cidx * LANES:(cidx + 1) * LANES]
        partner = jnp.where(is_first, pltpu.roll(tc, LANES - 16, axis=1), pltpu.roll(tc, 16, axis=1))
        o_ref[0, :, cidx * LANES:(cidx + 1) * LANES] = (tc * cos + partner * sin).astype(o_ref.dtype)


def _inproj_kernel(*refs, rope, v_transposed, with_qz, q_scale):
    it = iter(refs)
    x_ref, sc_ref, sh_ref = next(it), next(it), next(it)
    wq_ref = next(it) if with_qz else None
    wk_ref, wv_ref = next(it), next(it)
    wz_ref = next(it) if with_qz else None
    cos_ref, sin_ref = (next(it), next(it)) if rope else (None, None)
    q_ref = next(it) if with_qz else None
    k_ref, v_ref = next(it), next(it)
    z_ref = next(it) if with_qz else None

    h = (x_ref[0] * (1.0 + sc_ref[0]) + sh_ref[0]).astype(BF16)
    if rope:
        cos, sin = cos_ref[...], sin_ref[...]
        lane = lax.broadcasted_iota(jnp.int32, cos.shape, 1)
        is_first = (lane // 16) % 2 == 0
    if with_qz:
        q = jnp.dot(h, wq_ref[...], preferred_element_type=F32) * q_scale
        if rope:
            _rope_store(q, cos, sin, is_first, q_ref)
        else:
            q_ref[0] = q.astype(q_ref.dtype)
    k = jnp.dot(h, wk_ref[...], preferred_element_type=F32)
    if rope:
        _rope_store(k, cos, sin, is_first, k_ref)
    else:
        k_ref[0] = k.astype(k_ref.dtype)
    if v_transposed:
        v_ref[0] = lax.dot_general(wv_ref[...], h, (((1,), (1,)), ((), ())),
                                   preferred_element_type=F32).astype(v_ref.dtype)
    else:
        v_ref[0] = jnp.dot(h, wv_ref[...], preferred_element_type=F32).astype(v_ref.dtype)
    if with_qz:
        z_ref[0] = jnp.dot(h, wz_ref[...], preferred_element_type=F32).astype(z_ref.dtype)


def _inproj(x, sc, sh, wq, wk, wv, wz, *, tm, rope_tables, v_transposed, with_qz, q_scale):
    b, n, d = x.shape
    w = wk.shape[1]
    per_batch = sc.shape[0] == b
    mod_map = (lambda i, t: (i, 0, 0)) if per_batch else (lambda i, t: (0, 0, 0))
    const = lambda i, t: (0, 0)
    rope = rope_tables is not None
    wspec = pl.BlockSpec((d, w), const)
    row_spec = pl.BlockSpec((1, tm, w), lambda i, t: (i, t, 0))
    row_shape = jax.ShapeDtypeStruct((b, n, w), BF16)

    args = [x, sc, sh]
    in_specs = [pl.BlockSpec((1, tm, d), lambda i, t: (i, t, 0)),
                pl.BlockSpec((1, 1, d), mod_map), pl.BlockSpec((1, 1, d), mod_map)]
    out_shape, out_specs = [], []
    if with_qz:
        args.append(wq); in_specs.append(wspec)
        out_shape.append(row_shape); out_specs.append(row_spec)
    args += [wk, wv]
    in_specs += [wspec, pl.BlockSpec(wv.shape, const)]
    out_shape.append(row_shape); out_specs.append(row_spec)
    if v_transposed:
        out_shape.append(jax.ShapeDtypeStruct((b, w, n), BF16))
        out_specs.append(pl.BlockSpec((1, w, tm), lambda i, t: (i, 0, t)))
    else:
        out_shape.append(row_shape); out_specs.append(row_spec)
    if with_qz:
        args.append(wz); in_specs.append(wspec)
        out_shape.append(row_shape); out_specs.append(row_spec)
    if rope:
        args += list(rope_tables)
        in_specs += [pl.BlockSpec((tm, LANES), lambda i, t: (t, 0))] * 2

    outs = pl.pallas_call(
        functools.partial(_inproj_kernel, rope=rope, v_transposed=v_transposed,
                          with_qz=with_qz, q_scale=q_scale),
        grid=(b, n // tm),
        in_specs=in_specs,
        out_specs=out_specs,
        out_shape=out_shape,
        compiler_params=_cparams(("parallel", "parallel")),
        name="inproj",
    )(*args)
    if with_qz:
        return outs
    return None, outs[0], outs[1], None


def _stack_heads(q, first_head):
    zero = jnp.zeros_like(q)
    return jnp.concatenate([jnp.where(first_head, q, zero), jnp.where(first_head, zero, q)], axis=0)


_NT = (((1,), (1,)), ((), ()))


def _na_kernel(q_ref, k_ref, v_ref, kc_ref, vc_ref, bias_ref, o_ref, s_ref, m_ref, *, rows):
    kc = kc_ref[0]
    vc = vc_ref[0]
    first_head = lax.broadcasted_iota(jnp.int32, (GRID_W, LANES), 1) < HEAD_DIM
    band = NA_BAND_ROWS * GRID_W
    group = s_ref.shape[1] // (2 * GRID_W)
    n_groups = rows // group
    assert rows % group == 0 and n_groups % 2 == 0 and n_groups >= 4

    def band_start(r):
        r0 = jnp.clip(r - NA_BAND_ROWS // 2, 0, rows - NA_BAND_ROWS)
        return r0, pl.multiple_of(r0 * GRID_W, GRID_W)

    def stage_a(g, slot):
        for t in range(group):
            r = g * group + t
            r0, start = band_start(r)
            dr = r0 - r + NA_BAND_ROWS - 1
            q_rows = q_ref[0, pl.ds(pl.multiple_of(r * GRID_W, GRID_W), GRID_W), :]
            sq = _stack_heads(q_rows, first_head)
            s_loc = lax.dot_general(sq, k_ref[0, pl.ds(start, band), :], _NT, preferred_element_type=F32)
            bias = jnp.concatenate([bias_ref[0, dr + 2 * p] for p in range(NA_BAND_ROWS // 2)], axis=1)
            s_loc = s_loc + bias
            s_ctx = lax.dot_general(sq, kc, _NT, preferred_element_type=F32)
            m = jnp.maximum(jnp.max(s_loc, axis=1, keepdims=True), jnp.max(s_ctx, axis=1, keepdims=True))
            lo = t * 2 * GRID_W
            s_ref[slot, lo:lo + 2 * GRID_W, 0:band] = s_loc
            s_ref[slot, lo:lo + 2 * GRID_W, band:] = s_ctx
            m_ref[slot, lo:lo + 2 * GRID_W, :] = jnp.broadcast_to(m, (2 * GRID_W, LANES))

    def stage_b(g, slot):
        for t in range(group):
            r = g * group + t
            _, start = band_start(r)
            lo = t * 2 * GRID_W
            m = m_ref[slot, lo:lo + 2 * GRID_W, :]
            n_tiles = s_ref.shape[2] // LANES
            p = jnp.exp2(s_ref[slot, lo:lo + 2 * GRID_W, :] - jnp.concatenate([m] * n_tiles, axis=1))
            denom = jnp.sum(p, axis=1, keepdims=True)
            pb = p.astype(BF16)
            o2 = (jnp.dot(pb[:, :band], v_ref[0, pl.ds(start, band), :], preferred_element_type=F32)
                  + jnp.dot(pb[:, band:], vc, preferred_element_type=F32)) / denom
            y = jnp.where(first_head, o2[:GRID_W], o2[GRID_W:])
            o_ref[0, pl.ds(pl.multiple_of(r * GRID_W, GRID_W), GRID_W), :] = y.astype(o_ref.dtype)

    stage_a(0, 0)

    def body(i, carry):
        stage_a(2 * i + 1, 1)
        stage_b(2 * i, 0)
        stage_a(2 * i + 2, 0)
        stage_b(2 * i + 1, 1)
        return carry

    lax.fori_loop(0, n_groups // 2 - 1, body, 0)
    stage_a(n_groups - 1, 1)
    stage_b(n_groups - 2, 0)
    stage_b(n_groups - 1, 1)


def _na_attention(q, k, v, kc, vc, bias):
    b, n, w = q.shape
    c = kc.shape[1]
    rows = n // GRID_W
    hp = w // LANES
    tok = lambda i, j: (i, 0, j)
    return pl.pallas_call(
        functools.partial(_na_kernel, rows=rows),
        grid=(b, hp),
        in_specs=[
            pl.BlockSpec((1, n, LANES), tok), pl.BlockSpec((1, n, LANES), tok),
            pl.BlockSpec((1, n, LANES), tok),
            pl.BlockSpec((1, c, LANES), tok), pl.BlockSpec((1, c, LANES), tok),
            pl.BlockSpec((1,) + bias.shape[1:], lambda i, j: (j, 0, 0, 0)),
        ],
        out_specs=pl.BlockSpec((1, n, LANES), tok),
        out_shape=jax.ShapeDtypeStruct((b, n, w), BF16),
        scratch_shapes=[pltpu.VMEM((2, NA_GROUP_ROWS * 2 * GRID_W, NA_BAND_ROWS * GRID_W + c), F32),
                        pltpu.VMEM((2, NA_GROUP_ROWS * 2 * GRID_W, LANES), F32)],
        compiler_params=_cparams(("parallel", "parallel")),
        name="na_attention",
    )(q, k, v, kc, vc, bias)


def _na_ctx_kernel(q_ref, kc_ref, vc_ref, o_ref):
    q = q_ref[0]
    n = q.shape[0]
    first_head = lax.broadcasted_iota(jnp.int32, q.shape, 1) < HEAD_DIM
    s = lax.dot_general(_stack_heads(q, first_head), kc_ref[0], _NT, preferred_element_type=F32)
    m = jnp.max(s, axis=1, keepdims=True)
    p = jnp.exp2(s - m)
    o2 = jnp.dot(p.astype(BF16), vc_ref[0], preferred_element_type=F32) / jnp.sum(p, axis=1, keepdims=True)
    o_ref[0] = jnp.where(first_head, o2[:n], o2[n:]).astype(o_ref.dtype)


def _na_ctx_attention(qc, kc, vc):
    b, c, w = qc.shape
    tok = pl.BlockSpec((1, c, LANES), lambda i, j: (i, 0, j))
    return pl.pallas_call(
        _na_ctx_kernel,
        grid=(b, w // LANES),
        in_specs=[tok, tok, tok],
        out_specs=tok,
        out_shape=jax.ShapeDtypeStruct((b, c, w), BF16),
        compiler_params=_cparams(("parallel", "parallel")),
        name="na_ctx_attention",
    )(qc, kc, vc)


def _na_bias_table(rpb):
    h = rpb.shape[0]
    wq = jnp.arange(GRID_W)[:, None]
    kj = jnp.arange(GRID_W)[None, :]
    col_start = jnp.clip(wq - NA_WIN_COLS // 2, 0, GRID_W - NA_WIN_COLS)
    valid = (kj >= col_start) & (kj < col_start + NA_WIN_COLS)
    rel = jnp.clip(kj - wq + NA_WIN_COLS - 1, 0, 2 * NA_WIN_COLS - 2)
    t = jnp.where(valid[None, None], rpb[:, :, rel], NEG)
    t = t.reshape(h // 2, 2, 2 * NA_BAND_ROWS - 1, GRID_W, GRID_W)
    t = jnp.concatenate([t[:, 0], t[:, 1]], axis=2)
    return jnp.concatenate([t[:, :-1], t[:, 1:]], axis=3).astype(F32)


def _diff_kernel(*refs, n_lat, kb, tq, lambda_init):
    it = iter(refs)
    q_ref = next(it)
    k_ref, vt_ref = (next(it), next(it)) if n_lat else (None, None)
    kc_ref, vct_ref, lam_ref, g_ref, o_ref, acc_ref, s_ref = (next(it) for _ in range(7))
    n_ctx = kc_ref.shape[1]
    n_tiles = q_ref.shape[1] // tq
    first_map = lax.broadcasted_iota(jnp.int32, (tq, LANES), 1) < HEAD_DIM
    lp = lam_ref[...]
    lam = (jnp.exp(jnp.sum(lp[0:1] * lp[1:2], axis=1, keepdims=True))
           - jnp.exp(jnp.sum(lp[2:3] * lp[3:4], axis=1, keepdims=True)) + lambda_init)

    def kblock(j):
        return k_ref[0, pl.ds(pl.multiple_of(j * kb, kb), kb), :]

    def vblock(j):
        return vt_ref[0, :, pl.ds(pl.multiple_of(j * kb, kb), kb)]

    for t in range(n_tiles):
        rows = slice(t * tq, (t + 1) * tq)
        qcat = _stack_heads(q_ref[0, rows, :], first_map)
        acc_ref[t] = jnp.zeros(acc_ref.shape[1:], F32)

        def scores(kblk, slot, t=t, qcat=qcat):
            st = lax.dot_general(kblk, qcat, _NT, preferred_element_type=F32)
            s_ref[2 * t + slot, 0:kblk.shape[0], :] = st
            return jnp.max(st, axis=0, keepdims=True)

        def consume(slot, vtblk, bmax, m_old, t=t):
            m_new = jnp.maximum(m_old, bmax)
            alpha = jnp.exp2(m_old - m_new)
            p = jnp.exp2(s_ref[2 * t + slot, 0:vtblk.shape[1], :] - m_new)
            vaug = jnp.concatenate([vtblk, jnp.ones((SUBLANES_BF16, vtblk.shape[1]), BF16)], axis=0)
            acc_ref[t] = alpha * acc_ref[t] + jnp.dot(vaug, p.astype(BF16), preferred_element_type=F32)
            return m_new

        m = jnp.full((1, 2 * tq), NEG, F32)
        if n_lat:
            nb = n_lat // kb
            assert nb >= 2 and nb % 2 == 0 and n_ctx <= kb
            bm_a = scores(kblock(0), 0)

            def body(i, carry, scores=scores, consume=consume):
                m, bm_a = carry
                bm_b = scores(kblock(2 * i + 1), 1)
                m = consume(0, vblock(2 * i), bm_a, m)
                bm_a = scores(kblock(2 * i + 2), 0)
                m = consume(1, vblock(2 * i + 1), bm_b, m)
                return m, bm_a

            m, bm_a = lax.fori_loop(0, nb // 2 - 1, body, (m, bm_a))
            bm_b = scores(kblock(nb - 1), 1)
            m = consume(0, vblock(nb - 2), bm_a, m)
            bm_a = scores(kc_ref[0], 0)
            m = consume(1, vblock(nb - 1), bm_b, m)
        else:
            bm_a = scores(kc_ref[0], 0)
        consume(0, vct_ref[0], bm_a, m)

        o_all = acc_ref[t, 0:LANES, :] / acc_ref[t, LANES:LANES + 1, :]
        ot = o_all[:, :tq] - lam * o_all[:, tq:]
        ms = jnp.mean(ot * ot, axis=0, keepdims=True)
        ot = ot * lax.rsqrt(ms + LN_EPS)
        o_ref[0, rows, :] = (ot.T * g_ref[...] * (1.0 - lambda_init)).astype(o_ref.dtype)


def _diff_attention(q, k, vt, kc, vct, lam_params, subln_g, *, lambda_init, tq, kb, tiles=1):
    b, nq, w = q.shape
    c = kc.shape[1]
    heads = w // LANES
    n_lat = 0 if k is None else k.shape[1]
    step_rows = tiles * tq
    args = [q]
    in_specs = [pl.BlockSpec((1, step_rows, LANES), lambda i, h, t: (i, t, h))]
    if n_lat:
        args += [k, vt]
        in_specs += [pl.BlockSpec((1, n_lat, LANES), lambda i, h, t: (i, 0, h)),
                     pl.BlockSpec((1, LANES, n_lat), lambda i, h, t: (i, h, 0))]
    args += [kc, vct, lam_params, subln_g.reshape(1, LANES)]
    in_specs += [pl.BlockSpec((1, c, LANES), lambda i, h, t: (i, 0, h)),
                 pl.BlockSpec((1, LANES, c), lambda i, h, t: (i, h, 0)),
                 pl.BlockSpec(lam_params.shape, lambda i, h, t: (0, 0)),
                 pl.BlockSpec((1, LANES), lambda i, h, t: (0, 0))]
    return pl.pallas_call(
        functools.partial(_diff_kernel, n_lat=n_lat, kb=kb, tq=tq, lambda_init=lambda_init),
        grid=(b, heads, nq // step_rows),
        in_specs=in_specs,
        out_specs=pl.BlockSpec((1, step_rows, LANES), lambda i, h, t: (i, t, h)),
        out_shape=jax.ShapeDtypeStruct((b, nq, w), BF16),
        scratch_shapes=[pltpu.VMEM((tiles, LANES + SUBLANES_BF16, 2 * tq), F32),
                        pltpu.VMEM((2 * tiles, kb, 2 * tq), F32)],
        compiler_params=_cparams(("parallel", "parallel", "arbitrary")),
        name="diff_attention",
    )(*args)


def _outproj_kernel(y_ref, z_ref, x_ref, g_ref, w_ref, lng_ref, lnb_ref, o_ref, *, alpha):
    z = z_ref[0].astype(F32)
    u = (y_ref[0].astype(F32) * (z * jax.nn.sigmoid(z))).astype(BF16)
    out = jnp.dot(u, w_ref[...], preferred_element_type=F32)
    r = alpha * x_ref[0] + (1.0 + g_ref[0]) * out
    mu = jnp.mean(r, axis=1, keepdims=True)
    rc = r - mu
    var = jnp.mean(rc * rc, axis=1, keepdims=True)
    o_ref[0] = rc * lax.rsqrt(var + LN_EPS) * lng_ref[...] + lnb_ref[...]


def _outproj(y, z, x, gate, w_out, ln_g, ln_b, *, tm, alpha):
    b, n, d = x.shape
    w = y.shape[2]
    per_batch = gate.shape[0] == b
    mod_map = (lambda i, t: (i, 0, 0)) if per_batch else (lambda i, t: (0, 0, 0))
    row = lambda i, t: (i, t, 0)
    const = lambda i, t: (0, 0)
    return pl.pallas_call(
        functools.partial(_outproj_kernel, alpha=alpha),
        grid=(b, n // tm),
        in_specs=[
            pl.BlockSpec((1, tm, w), row), pl.BlockSpec((1, tm, w), row),
            pl.BlockSpec((1, tm, d), row), pl.BlockSpec((1, 1, d), mod_map),
            pl.BlockSpec((w, d), const), pl.BlockSpec((1, d), const), pl.BlockSpec((1, d), const),
        ],
        out_specs=pl.BlockSpec((1, tm, d), row),
        out_shape=jax.ShapeDtypeStruct((b, n, d), F32),
        compiler_params=_cparams(("parallel", "parallel")),
        name="outproj_ln",
    )(y, z, x, gate, w_out, ln_g.reshape(1, d), ln_b.reshape(1, d))


def _rope_tables(n_tok):
    t = jnp.arange(n_tok, dtype=jnp.int32)
    row = (t // GRID_W).astype(F32)[:, None]
    col = (t % GRID_W).astype(F32)[:, None]
    n_freq = HEAD_DIM // 4
    inv_freq = ROPE_BASE ** (-jnp.arange(n_freq, dtype=F32) / n_freq)
    lane = jnp.arange(LANES)
    pos = jnp.where(((lane % HEAD_DIM) // (HEAD_DIM // 2) == 0)[None, :], row, col)
    ang = pos * inv_freq[lane % n_freq][None, :]
    first = ((lane // n_freq) % 2 == 0)[None, :]
    return jnp.cos(ang), jnp.where(first, -jnp.sin(ang), jnp.sin(ang))


def kernel(x, c, ctx, c_ctx, w_mod, b_mod, w_in, w_out, ln_g, ln_b, na_rpb, diff_lambda, diff_subln_g):
    depth, d, _ = w_mod.shape
    b, n, _ = x.shape
    w = w_out.shape[1]
    n_ctx = ctx.shape[1]
    alpha = (2.0 * depth) ** 0.25
    q_scale = HEAD_DIM ** -0.5 * LOG2E

    cond = jnp.concatenate([c, c_ctx[None, :], jnp.zeros((16 - b - 1, d), F32)], axis=0)
    mods = _modulation(cond, w_mod, b_mod)
    w_in_bf = w_in.astype(BF16)
    w_out_bf = w_out.astype(BF16)
    rope_tables = _rope_tables(n)

    for l in range(depth):
        need_ctx = l < depth - 1
        is_diff = l % 2 == 1
        m = mods[l]
        sh, sc, g = (m[:b, None, i * d:(i + 1) * d] for i in range(3))
        shc, scc, gc = (m[b:b + 1, None, i * d:(i + 1) * d] for i in range(3))
        wq, wk, wv, wz = (w_in_bf[l, :, i * w:(i + 1) * w] for i in range(4))
        if is_diff:
            wv = wv.T
        q, k, v, z = _inproj(x, sc, sh, wq, wk, wv, wz, tm=512,
                             rope_tables=rope_tables if is_diff else None,
                             v_transposed=is_diff, with_qz=True, q_scale=q_scale)
        qc, kc, vc, zc = _inproj(ctx, scc, shc, wq, wk, wv, wz, tm=n_ctx, rope_tables=None,
                                 v_transposed=is_diff, with_qz=need_ctx, q_scale=q_scale)
        if is_diff:
            lam_p, sub_g = diff_lambda[l // 2], diff_subln_g[l // 2]
            lambda_init = 0.8 - 0.6 * math.exp(-0.3 * l)
            y = _diff_attention(q, k, v, kc, vc, lam_p, sub_g, lambda_init=lambda_init, tq=512, kb=512, tiles=4)
            if need_ctx:
                yc = _diff_attention(qc, None, None, kc, vc, lam_p, sub_g,
                                     lambda_init=lambda_init, tq=n_ctx, kb=512)
        else:
            y = _na_attention(q, k, v, kc, vc, _na_bias_table(na_rpb[l // 2] * LOG2E))
            if need_ctx:
                yc = _na_ctx_attention(qc, kc, vc)
        x_new = _outproj(y, z, x, g, w_out_bf[l], ln_g[l], ln_b[l], tm=512, alpha=alpha)
        if need_ctx:
            ctx = _outproj(yc, zc, ctx, gc, w_out_bf[l], ln_g[l], ln_b[l], tm=n_ctx, alpha=alpha)
        x = x_new
    return x
```

```python
import functools
import math

import jax
import jax.numpy as jnp
from jax import lax
from jax.experimental import pallas as pl
from jax.experimental.pallas import tpu as pltpu

F32 = jnp.float32
BF16 = jnp.bfloat16

GRID_W = 64
ROPE_BASE = 10000.0
LN_EPS = 1e-5
NEG = -1e30
LANES = 128
HEAD_DIM = 64
NA_BAND_ROWS = 8
NA_WIN_COLS = 16
NA_GROUP_ROWS = 2
SUBLANES_BF16 = 16
DIFF_SLOTS = 3
LOG2E = math.log2(math.e)
VMEM_LIMIT = 48 * 1024 * 1024


def _cparams(sem):
    return pltpu.CompilerParams(dimension_semantics=sem, vmem_limit_bytes=VMEM_LIMIT)


def _mod_kernel(c_ref, w_ref, b_ref, o_ref):
    c = c_ref[...]
    s = c * jax.nn.sigmoid(c)
    o_ref[0] = jnp.dot(s, w_ref[0], preferred_element_type=F32) + b_ref[0]


def _modulation(cond, w_mod, b_mod):
    depth, d, d3 = w_mod.shape
    rows = cond.shape[0]
    tn = 1024
    return pl.pallas_call(
        _mod_kernel,
        grid=(depth, d3 // tn),
        in_specs=[
            pl.BlockSpec((rows, d), lambda l, j: (0, 0)),
            pl.BlockSpec((1, d, tn), lambda l, j: (l, 0, j)),
            pl.BlockSpec((1, 1, tn), lambda l, j: (l, 0, j)),
        ],
        out_specs=pl.BlockSpec((1, rows, tn), lambda l, j: (l, 0, j)),
        out_shape=jax.ShapeDtypeStruct((depth, rows, d3), F32),
        compiler_params=_cparams(("parallel", "parallel")),
        name="modulation",
    )(cond, w_mod, b_mod.reshape(depth, 1, d3))


def _rope_store(t, cos, sin, is_first, o_ref):
    for cidx in range(t.shape[1] // LANES):
        tc = t[:, cidx * LANES:(cidx + 1) * LANES]
        partner = jnp.where(is_first, pltpu.roll(tc, LANES - 16, axis=1), pltpu.roll(tc, 16, axis=1))
        o_ref[0, :, cidx * LANES:(cidx + 1) * LANES] = (tc * cos + partner * sin).astype(o_ref.dtype)


def _inproj_kernel(*refs, rope, v_transposed, with_qz, q_scale):
    it = iter(refs)
    x_ref, sc_ref, sh_ref = next(it), next(it), next(it)
    wq_ref = next(it) if with_qz else None
    wk_ref, wv_ref = next(it), next(it)
    wz_ref = next(it) if with_qz else None
    cos_ref, sin_ref = (next(it), next(it)) if rope else (None, None)
    q_ref = next(it) if with_qz else None
    k_ref, v_ref = next(it), next(it)
    z_ref = next(it) if with_qz else None

    h = (x_ref[0] * (1.0 + sc_ref[0]) + sh_ref[0]).astype(BF16)
    if rope:
        cos, sin = cos_ref[...], sin_ref[...]
        lane = lax.broadcasted_iota(jnp.int32, cos.shape, 1)
        is_first = (lane // 16) % 2 == 0
    if with_qz:
        q = jnp.dot(h, wq_ref[...], preferred_element_type=F32) * q_scale
        if rope:
            _rope_store(q, cos, sin, is_first, q_ref)
        else:
            q_ref[0] = q.astype(q_ref.dtype)
    k = jnp.dot(h, wk_ref[...], preferred_element_type=F32)
    if rope:
        _rope_store(k, cos, sin, is_first, k_ref)
    else:
        k_ref[0] = k.astype(k_ref.dtype)
    if v_transposed:
        v_ref[0] = lax.dot_general(wv_ref[...], h, (((1,), (1,)), ((), ())),
                                   preferred_element_type=F32).astype(v_ref.dtype)
    else:
        v_ref[0] = jnp.dot(h, wv_ref[...], preferred_element_type=F32).astype(v_ref.dtype)
    if with_qz:
        z_ref[0] = jnp.dot(h, wz_ref[...], preferred_element_type=F32).astype(z_ref.dtype)


def _inproj(x, sc, sh, w_all, layer, wv_t, *, tm, rope_tables, v_transposed, with_qz, q_scale):
    b, n, d = x.shape
    w = w_all.shape[2] // 4
    per_batch = sc.shape[0] == b
    mod_map = (lambda i, t: (i, 0, 0)) if per_batch else (lambda i, t: (0, 0, 0))
    const = lambda i, t: (0, 0)
    rope = rope_tables is not None
    wspec = lambda col: pl.BlockSpec((None, d, w), lambda i, t: (layer, 0, col))
    row_spec = pl.BlockSpec((1, tm, w), lambda i, t: (i, t, 0))
    row_shape = jax.ShapeDtypeStruct((b, n, w), BF16)

    args = [x, sc, sh]
    in_specs = [pl.BlockSpec((1, tm, d), lambda i, t: (i, t, 0)),
                pl.BlockSpec((1, 1, d), mod_map), pl.BlockSpec((1, 1, d), mod_map)]
    out_shape, out_specs = [], []
    if with_qz:
        args.append(w_all); in_specs.append(wspec(0))
        out_shape.append(row_shape); out_specs.append(row_spec)
    args += [w_all, wv_t if v_transposed else w_all]
    in_specs += [wspec(1), pl.BlockSpec(wv_t.shape, const) if v_transposed else wspec(2)]
    out_shape.append(row_shape); out_specs.append(row_spec)
    if v_transposed:
        out_shape.append(jax.ShapeDtypeStruct((b, w, n), BF16))
        out_specs.append(pl.BlockSpec((1, w, tm), lambda i, t: (i, 0, t)))
    else:
        out_shape.append(row_shape); out_specs.append(row_spec)
    if with_qz:
        args.append(w_all); in_specs.append(wspec(3))
        out_shape.append(row_shape); out_specs.append(row_spec)
    if rope:
        args += list(rope_tables)
        in_specs += [pl.BlockSpec((tm, LANES), lambda i, t: (t, 0))] * 2

    outs = pl.pallas_call(
        functools.partial(_inproj_kernel, rope=rope, v_transposed=v_transposed,
                          with_qz=with_qz, q_scale=q_scale),
        grid=(b, n // tm),
        in_specs=in_specs,
        out_specs=out_specs,
        out_shape=out_shape,
        compiler_params=_cparams(("parallel", "parallel")),
        name="inproj",
    )(*args)
    if with_qz:
        return outs
    return None, outs[0], outs[1], None


def _stack_heads(q, first_head):
    zero = jnp.zeros_like(q)
    return jnp.concatenate([jnp.where(first_head, q, zero), jnp.where(first_head, zero, q)], axis=0)


_NT = (((1,), (1,)), ((), ()))


def _na_kernel(q_ref, k_ref, v_ref, kc_ref, vc_ref, bias_ref, o_ref, s_ref, *, rows):
    kc = kc_ref[0]
    vc = vc_ref[0]
    first_head = lax.broadcasted_iota(jnp.int32, (GRID_W, LANES), 1) < HEAD_DIM
    band = NA_BAND_ROWS * GRID_W
    group = s_ref.shape[1] // (2 * GRID_W)
    n_groups = rows // group
    assert rows % group == 0 and n_groups % 2 == 0 and n_groups >= 4

    def band_start(r):
        r0 = jnp.clip(r - NA_BAND_ROWS // 2, 0, rows - NA_BAND_ROWS)
        return r0, pl.multiple_of(r0 * GRID_W, GRID_W)

    def stage_a(g, slot):
        for t in range(group):
            r = g * group + t
            r0, start = band_start(r)
            dr = r0 - r + NA_BAND_ROWS - 1
            q_rows = q_ref[0, pl.ds(pl.multiple_of(r * GRID_W, GRID_W), GRID_W), :]
            sq = _stack_heads(q_rows, first_head)
            s_loc = lax.dot_general(sq, k_ref[0, pl.ds(start, band), :], _NT, preferred_element_type=F32)
            bias = jnp.concatenate([bias_ref[0, dr + 2 * p] for p in range(NA_BAND_ROWS // 2)], axis=1)
            s_loc = s_loc + bias
            s_ctx = lax.dot_general(sq, kc, _NT, preferred_element_type=F32)
            m = jnp.maximum(jnp.max(s_loc, axis=1, keepdims=True), jnp.max(s_ctx, axis=1, keepdims=True))
            lo = t * 2 * GRID_W
            s_ref[slot, lo:lo + 2 * GRID_W, 0:band] = s_loc - m
            s_ref[slot, lo:lo + 2 * GRID_W, band:] = s_ctx - m

    def stage_b(g, slot):
        for t in range(group):
            r = g * group + t
            _, start = band_start(r)
            lo = t * 2 * GRID_W
            p = jnp.exp2(s_ref[slot, lo:lo + 2 * GRID_W, :])
            denom = jnp.sum(p, axis=1, keepdims=True)
            pb = p.astype(BF16)
            o2 = (jnp.dot(pb[:, :band], v_ref[0, pl.ds(start, band), :], preferred_element_type=F32)
                  + jnp.dot(pb[:, band:], vc, preferred_element_type=F32)) / denom
            y = jnp.where(first_head, o2[:GRID_W], o2[GRID_W:])
            o_ref[0, pl.ds(pl.multiple_of(r * GRID_W, GRID_W), GRID_W), :] = y.astype(o_ref.dtype)

    stage_a(0, 0)

    def body(i, carry):
        stage_a(2 * i + 1, 1)
        stage_b(2 * i, 0)
        stage_a(2 * i + 2, 0)
        stage_b(2 * i + 1, 1)
        return carry

    lax.fori_loop(0, n_groups // 2 - 1, body, 0)
    stage_a(n_groups - 1, 1)
    stage_b(n_groups - 2, 0)
    stage_b(n_groups - 1, 1)


def _na_attention(q, k, v, kc, vc, bias):
    b, n, w = q.shape
    c = kc.shape[1]
    rows = n // GRID_W
    hp = w // LANES
    tok = lambda i, j: (i, 0, j)
    return pl.pallas_call(
        functools.partial(_na_kernel, rows=rows),
        grid=(b, hp),
        in_specs=[
            pl.BlockSpec((1, n, LANES), tok), pl.BlockSpec((1, n, LANES), tok),
            pl.BlockSpec((1, n, LANES), tok),
            pl.BlockSpec((1, c, LANES), tok), pl.BlockSpec((1, c, LANES), tok),
            pl.BlockSpec((1,) + bias.shape[1:], lambda i, j: (j, 0, 0, 0)),
        ],
        out_specs=pl.BlockSpec((1, n, LANES), tok),
        out_shape=jax.ShapeDtypeStruct((b, n, w), BF16),
        scratch_shapes=[pltpu.VMEM((2, NA_GROUP_ROWS * 2 * GRID_W, NA_BAND_ROWS * GRID_W + c), F32)],
        compiler_params=_cparams(("parallel", "parallel")),
        name="na_attention",
    )(q, k, v, kc, vc, bias)


def _na_ctx_kernel(q_ref, kc_ref, vc_ref, o_ref):
    n = q_ref.shape[1]
    first_head = lax.broadcasted_iota(jnp.int32, (n, LANES), 1) < HEAD_DIM
    for hp in range(q_ref.shape[2] // LANES):
        lanes = slice(hp * LANES, (hp + 1) * LANES)
        s = lax.dot_general(_stack_heads(q_ref[0, :, lanes], first_head), kc_ref[0, :, lanes], _NT,
                            preferred_element_type=F32)
        m = jnp.max(s, axis=1, keepdims=True)
        p = jnp.exp2(s - m)
        o2 = (jnp.dot(p.astype(BF16), vc_ref[0, :, lanes], preferred_element_type=F32)
              / jnp.sum(p, axis=1, keepdims=True))
        o_ref[0, :, lanes] = jnp.where(first_head, o2[:n], o2[n:]).astype(o_ref.dtype)


def _na_ctx_attention(qc, kc, vc):
    b, c, w = qc.shape
    tok = pl.BlockSpec((1, c, w), lambda i: (i, 0, 0))
    return pl.pallas_call(
        _na_ctx_kernel,
        grid=(b,),
        in_specs=[tok, tok, tok],
        out_specs=tok,
        out_shape=jax.ShapeDtypeStruct((b, c, w), BF16),
        compiler_params=_cparams(("parallel",)),
        name="na_ctx_attention",
    )(qc, kc, vc)


def _na_bias_table(rpb):
    h = rpb.shape[0]
    wq = jnp.arange(GRID_W)[:, None]
    kj = jnp.arange(GRID_W)[None, :]
    col_start = jnp.clip(wq - NA_WIN_COLS // 2, 0, GRID_W - NA_WIN_COLS)
    valid = (kj >= col_start) & (kj < col_start + NA_WIN_COLS)
    rel = kj - wq + NA_WIN_COLS - 1
    onehot = rel[None] == jnp.arange(2 * NA_WIN_COLS - 1)[:, None, None]
    t = jnp.sum(jnp.where(onehot[None, None], rpb[:, :, :, None, None], 0.0), axis=2)
    t = jnp.where(valid[None, None], t, NEG)
    t = t.reshape(h // 2, 2, 2 * NA_BAND_ROWS - 1, GRID_W, GRID_W)
    t = jnp.concatenate([t[:, 0], t[:, 1]], axis=2)
    return jnp.concatenate([t[:, :-1], t[:, 1:]], axis=3).astype(F32)


def _diff_kernel(*refs, n_lat, kb, tq, lambda_init):
    it = iter(refs)
    q_ref = next(it)
    k_ref, vt_ref = (next(it), next(it)) if n_lat else (None, None)
    kc_ref, vct_ref, lam_ref, g_ref, o_ref, acc_ref, s_ref = (next(it) for _ in range(7))
    n_ctx = kc_ref.shape[1]
    n_tiles = q_ref.shape[1] // tq
    first_map = lax.broadcasted_iota(jnp.int32, (tq, LANES), 1) < HEAD_DIM
    lp = lam_ref[...]
    lam = (jnp.exp(jnp.sum(lp[0:1] * lp[1:2], axis=1, keepdims=True))
           - jnp.exp(jnp.sum(lp[2:3] * lp[3:4], axis=1, keepdims=True)) + lambda_init)

    def kblock(j):
        return k_ref[0, pl.ds(pl.multiple_of(j * kb, kb), kb), :]

    def vblock(j):
        return vt_ref[0, :, pl.ds(pl.multiple_of(j * kb, kb), kb)]

    for t in range(n_tiles):
        rows = slice(t * tq, (t + 1) * tq)
        qcat = _stack_heads(q_ref[0, rows, :], first_map)
        acc_ref[t] = jnp.zeros(acc_ref.shape[1:], F32)

        def scores(kblk, slot, t=t, qcat=qcat):
            st = lax.dot_general(kblk, qcat, _NT, preferred_element_type=F32)
            s_ref[DIFF_SLOTS * t + slot, 0:kblk.shape[0], :] = st
            return jnp.max(st, axis=0, keepdims=True)

        def consume(slot, vtblk, bmax, m_old, t=t):
            m_new = jnp.maximum(m_old, bmax)
            alpha = jnp.exp2(m_old - m_new)
            p = jnp.exp2(s_ref[DIFF_SLOTS * t + slot, 0:vtblk.shape[1], :] - m_new)
            vaug = jnp.concatenate([vtblk, jnp.ones((SUBLANES_BF16, vtblk.shape[1]), BF16)], axis=0)
            acc_ref[t] = alpha * acc_ref[t] + jnp.dot(vaug, p.astype(BF16), preferred_element_type=F32)
            return m_new

        m = jnp.full((1, 2 * tq), NEG, F32)
        if n_lat:
            nb = n_lat // kb
            assert nb >= 2 and (nb - 2) % DIFF_SLOTS == 0 and n_ctx <= kb
            bm = scores(kblock(0), 0)

            def body(i, carry, scores=scores, consume=consume):
                m, bm = carry
                for u in range(DIFF_SLOTS):
                    j = DIFF_SLOTS * i + u
                    bm_next = scores(kblock(j + 1), (u + 1) % DIFF_SLOTS)
                    m = consume(u, vblock(j), bm, m)
                    bm = bm_next
                return m, bm

            m, bm = lax.fori_loop(0, (nb - 2) // DIFF_SLOTS, body, (m, bm))
            bm_next = scores(kblock(nb - 1), (nb - 1) % DIFF_SLOTS)
            m = consume((nb - 2) % DIFF_SLOTS, vblock(nb - 2), bm, m)
            bm_ctx = scores(kc_ref[0], nb % DIFF_SLOTS)
            m = consume((nb - 1) % DIFF_SLOTS, vblock(nb - 1), bm_next, m)
            consume(nb % DIFF_SLOTS, vct_ref[0], bm_ctx, m)
        else:
            consume(0, vct_ref[0], scores(kc_ref[0], 0), m)

        o_all = acc_ref[t, 0:LANES, :] / acc_ref[t, LANES:LANES + 1, :]
        ot = o_all[:, :tq] - lam * o_all[:, tq:]
        ms = jnp.mean(ot * ot, axis=0, keepdims=True)
        ot = ot * lax.rsqrt(ms + LN_EPS)
        o_ref[0, rows, :] = (ot.T * g_ref[...] * (1.0 - lambda_init)).astype(o_ref.dtype)


def _diff_attention(q, k, vt, kc, vct, lam_params, subln_g, *, lambda_init, tq, kb, tiles=1):
    b, nq, w = q.shape
    c = kc.shape[1]
    heads = w // LANES
    n_lat = 0 if k is None else k.shape[1]
    step_rows = tiles * tq
    args = [q]
    in_specs = [pl.BlockSpec((1, step_rows, LANES), lambda i, h, t: (i, t, h))]
    if n_lat:
        args += [k, vt]
        in_specs += [pl.BlockSpec((1, n_lat, LANES), lambda i, h, t: (i, 0, h)),
                     pl.BlockSpec((1, LANES, n_lat), lambda i, h, t: (i, h, 0))]
    args += [kc, vct, lam_params, subln_g.reshape(1, LANES)]
    in_specs += [pl.BlockSpec((1, c, LANES), lambda i, h, t: (i, 0, h)),
                 pl.BlockSpec((1, LANES, c), lambda i, h, t: (i, h, 0)),
                 pl.BlockSpec(lam_params.shape, lambda i, h, t: (0, 0)),
                 pl.BlockSpec((1, LANES), lambda i, h, t: (0, 0))]
    return pl.pallas_call(
        functools.partial(_diff_kernel, n_lat=n_lat, kb=kb, tq=tq, lambda_init=lambda_init),
        grid=(b, heads, nq // step_rows),
        in_specs=in_specs,
        out_specs=pl.BlockSpec((1, step_rows, LANES), lambda i, h, t: (i, t, h)),
        out_shape=jax.ShapeDtypeStruct((b, nq, w), BF16),
        scratch_shapes=[pltpu.VMEM((tiles, LANES + SUBLANES_BF16, 2 * tq), F32),
                        pltpu.VMEM((DIFF_SLOTS * tiles, kb, 2 * tq), F32)],
        compiler_params=_cparams(("parallel", "parallel", "arbitrary")),
        name="diff_attention",
    )(*args)


def _outproj_kernel(y_ref, z_ref, x_ref, g_ref, w_ref, lng_ref, lnb_ref, o_ref, *, alpha):
    z = z_ref[0].astype(F32)
    u = (y_ref[0].astype(F32) * (z * jax.nn.sigmoid(z))).astype(BF16)
    out = jnp.dot(u, w_ref[...], preferred_element_type=F32)
    r = alpha * x_ref[0] + (1.0 + g_ref[0]) * out
    mu = jnp.mean(r, axis=1, keepdims=True)
    rc = r - mu
    var = jnp.mean(rc * rc, axis=1, keepdims=True)
    o_ref[0] = rc * lax.rsqrt(var + LN_EPS) * lng_ref[...] + lnb_ref[...]


def _outproj(y, z, x, gate, w_out_all, ln_g_all, ln_b_all, layer, *, tm, alpha):
    b, n, d = x.shape
    w = y.shape[2]
    per_batch = gate.shape[0] == b
    mod_map = (lambda i, t: (i, 0, 0)) if per_batch else (lambda i, t: (0, 0, 0))
    row = lambda i, t: (i, t, 0)
    lay = lambda i, t: (layer, 0, 0)
    return pl.pallas_call(
        functools.partial(_outproj_kernel, alpha=alpha),
        grid=(b, n // tm),
        in_specs=[
            pl.BlockSpec((1, tm, w), row), pl.BlockSpec((1, tm, w), row),
            pl.BlockSpec((1, tm, d), row), pl.BlockSpec((1, 1, d), mod_map),
            pl.BlockSpec((None, w, d), lay), pl.BlockSpec((None, 1, d), lay), pl.BlockSpec((None, 1, d), lay),
        ],
        out_specs=pl.BlockSpec((1, tm, d), row),
        out_shape=jax.ShapeDtypeStruct((b, n, d), F32),
        compiler_params=_cparams(("parallel", "parallel")),
        name="outproj_ln",
    )(y, z, x, gate, w_out_all, ln_g_all, ln_b_all)


def _rope_tables(n_tok):
    t = jnp.arange(n_tok, dtype=jnp.int32)
    row = (t // GRID_W).astype(F32)[:, None]
    col = (t % GRID_W).astype(F32)[:, None]
    n_freq = HEAD_DIM // 4
    inv_freq = ROPE_BASE ** (-jnp.arange(n_freq, dtype=F32) / n_freq)
    lane = jnp.arange(LANES)
    pos = jnp.where(((lane % HEAD_DIM) // (HEAD_DIM // 2) == 0)[None, :], row, col)
    ang = pos * inv_freq[lane % n_freq][None, :]
    first = ((lane // n_freq) % 2 == 0)[None, :]
    return jnp.cos(ang), jnp.where(first, -jnp.sin(ang), jnp.sin(ang))


def kernel(x, c, ctx, c_ctx, w_mod, b_mod, w_in, w_out, ln_g, ln_b, na_rpb, diff_lambda, diff_subln_g):
    depth, d, _ = w_mod.shape
    b, n, _ = x.shape
    w = w_out.shape[1]
    n_ctx = ctx.shape[1]
    alpha = (2.0 * depth) ** 0.25
    q_scale = HEAD_DIM ** -0.5 * LOG2E

    cond = jnp.concatenate([c, c_ctx[None, :], jnp.zeros((16 - b - 1, d), F32)], axis=0)
    mods = _modulation(cond, w_mod, b_mod)
    w_in_bf = w_in.astype(BF16)
    w_out_bf = w_out.astype(BF16)
    ln_g3, ln_b3 = ln_g[:, None, :], ln_b[:, None, :]
    rope_tables = _rope_tables(n)

    for l in range(depth):
        need_ctx = l < depth - 1
        is_diff = l % 2 == 1
        m = mods[l]
        sh, sc, g = (m[:b, None, i * d:(i + 1) * d] for i in range(3))
        shc, scc, gc = (m[b:b + 1, None, i * d:(i + 1) * d] for i in range(3))
        wv_t = w_in_bf[l, :, 2 * w:3 * w].T if is_diff else None
        q, k, v, z = _inproj(x, sc, sh, w_in_bf, l, wv_t, tm=512,
                             rope_tables=rope_tables if is_diff else None,
                             v_transposed=is_diff, with_qz=True, q_scale=q_scale)
        qc, kc, vc, zc = _inproj(ctx, scc, shc, w_in_bf, l, wv_t, tm=n_ctx, rope_tables=None,
                                 v_transposed=is_diff, with_qz=need_ctx, q_scale=q_scale)
        if is_diff:
            lam_p, sub_g = diff_lambda[l // 2], diff_subln_g[l // 2]
            lambda_init = 0.8 - 0.6 * math.exp(-0.3 * l)
            y = _diff_attention(q, k, v, kc, vc, lam_p, sub_g, lambda_init=lambda_init, tq=512, kb=512, tiles=4)
            if need_ctx:
                yc = _diff_attention(qc, None, None, kc, vc, lam_p, sub_g,
                                     lambda_init=lambda_init, tq=n_ctx, kb=512)
        else:
            y = _na_attention(q, k, v, kc, vc, _na_bias_table(na_rpb[l // 2] * LOG2E))
            if need_ctx:
                yc = _na_ctx_attention(qc, kc, vc)
        x_new = _outproj(y, z, x, g, w_out_bf, ln_g3, ln_b3, l, tm=512, alpha=alpha)
        if need_ctx:
            ctx = _outproj(yc, zc, ctx, gc, w_out_bf, ln_g3, ln_b3, l, tm=n_ctx, alpha=alpha)
        x = x_new
    return x
```

```python
import functools
import math

import jax
import jax.numpy as jnp
from jax import lax
from jax.experimental import pallas as pl
from jax.experimental.pallas import tpu as pltpu

F32 = jnp.float32
BF16 = jnp.bfloat16

GRID_W = 64
ROPE_BASE = 10000.0
LN_EPS = 1e-5
NEG = -1e30
LANES = 128
HEAD_DIM = 64
NA_BAND_ROWS = 8
NA_WIN_COLS = 16
NA_GROUP_ROWS = 2
SUBLANES_BF16 = 16
DIFF_SLOTS = 3
LOG2E = math.log2(math.e)
VMEM_LIMIT = 48 * 1024 * 1024


def _cparams(sem):
    return pltpu.CompilerParams(dimension_semantics=sem, vmem_limit_bytes=VMEM_LIMIT)


def _mod_kernel(c_ref, w_ref, b_ref, o_ref):
    c = c_ref[...]
    s = c * jax.nn.sigmoid(c)
    o_ref[0] = jnp.dot(s, w_ref[0], preferred_element_type=F32) + b_ref[0]


def _modulation(cond, w_mod, b_mod):
    depth, d, d3 = w_mod.shape
    rows = cond.shape[0]
    tn = 1024
    return pl.pallas_call(
        _mod_kernel,
        grid=(depth, d3 // tn),
        in_specs=[
            pl.BlockSpec((rows, d), lambda l, j: (0, 0)),
            pl.BlockSpec((1, d, tn), lambda l, j: (l, 0, j)),
            pl.BlockSpec((1, 1, tn), lambda l, j: (l, 0, j)),
        ],
        out_specs=pl.BlockSpec((1, rows, tn), lambda l, j: (l, 0, j)),
        out_shape=jax.ShapeDtypeStruct((depth, rows, d3), F32),
        compiler_params=_cparams(("parallel", "parallel")),
        name="modulation",
    )(cond, w_mod, b_mod.reshape(depth, 1, d3))


def _rope_store(t, cos, sin, is_first, o_ref):
    for cidx in range(t.shape[1] // LANES):
        tc = t[:, cidx * LANES:(cidx + 1) * LANES]
        partner = jnp.where(is_first, pltpu.roll(tc, LANES - 16, axis=1), pltpu.roll(tc, 16, axis=1))
        o_ref[0, :, cidx * LANES:(cidx + 1) * LANES] = (tc * cos + partner * sin).astype(o_ref.dtype)


def _outproj_value(y_ref, z_ref, x_ref, g_ref, w_ref, lng_ref, lnb_ref, alpha):
    z = z_ref[0].astype(F32)
    u = (y_ref[0].astype(F32) * (z * jax.nn.sigmoid(z))).astype(BF16)
    out = jnp.dot(u, w_ref[...], preferred_element_type=F32)
    r = alpha * x_ref[0] + (1.0 + g_ref[0]) * out
    mu = jnp.mean(r, axis=1, keepdims=True)
    rc = r - mu
    var = jnp.mean(rc * rc, axis=1, keepdims=True)
    return rc * lax.rsqrt(var + LN_EPS) * lng_ref[...] + lnb_ref[...]


def _inproj_kernel(*refs, rope, v_transposed, with_qz, q_scale, alpha):
    it = iter(refs)
    prev = [next(it) for _ in range(7)] if alpha is not None else None
    x_ref = prev[2] if prev else next(it)
    sc_ref, sh_ref = next(it), next(it)
    wq_ref = next(it) if with_qz else None
    wk_ref, wv_ref = next(it), next(it)
    wz_ref = next(it) if with_qz else None
    cos_ref, sin_ref = (next(it), next(it)) if rope else (None, None)
    xo_ref = next(it) if prev else None
    q_ref = next(it) if with_qz else None
    k_ref, v_ref = next(it), next(it)
    z_ref = next(it) if with_qz else None

    if prev:
        x_val = _outproj_value(*prev, alpha)
        xo_ref[0] = x_val
    else:
        x_val = x_ref[0]
    h = (x_val * (1.0 + sc_ref[0]) + sh_ref[0]).astype(BF16)
    if rope:
        cos, sin = cos_ref[...], sin_ref[...]
        lane = lax.broadcasted_iota(jnp.int32, cos.shape, 1)
        is_first = (lane // 16) % 2 == 0
    if with_qz:
        q = jnp.dot(h, wq_ref[...], preferred_element_type=F32) * q_scale
        if rope:
            _rope_store(q, cos, sin, is_first, q_ref)
        else:
            q_ref[0] = q.astype(q_ref.dtype)
    k = jnp.dot(h, wk_ref[...], preferred_element_type=F32)
    if rope:
        _rope_store(k, cos, sin, is_first, k_ref)
    else:
        k_ref[0] = k.astype(k_ref.dtype)
    if v_transposed:
        v_ref[0] = lax.dot_general(wv_ref[...], h, (((1,), (1,)), ((), ())),
                                   preferred_element_type=F32).astype(v_ref.dtype)
    else:
        v_ref[0] = jnp.dot(h, wv_ref[...], preferred_element_type=F32).astype(v_ref.dtype)
    if with_qz:
        z_ref[0] = jnp.dot(h, wz_ref[...], preferred_element_type=F32).astype(z_ref.dtype)


def _inproj(x, sc, sh, w_all, layer, wv_t, *, tm, rope_tables, v_transposed, with_qz, q_scale, prev=None):
    b, n, d = x.shape
    w = w_all.shape[2] // 4
    per_batch = sc.shape[0] == b
    mod_map = (lambda i, t: (i, 0, 0)) if per_batch else (lambda i, t: (0, 0, 0))
    const = lambda i, t: (0, 0)
    rope = rope_tables is not None
    once = dict(pipeline_mode=pl.Buffered(1))
    wspec = lambda col: pl.BlockSpec((None, d, w), lambda i, t: (layer, 0, col), **once)
    row = lambda i, t: (i, t, 0)
    row_spec = pl.BlockSpec((1, tm, w), row)
    row_shape = jax.ShapeDtypeStruct((b, n, w), BF16)

    out_shape, out_specs = [], []
    if prev is None:
        args, in_specs, alpha = [x], [pl.BlockSpec((1, tm, d), row)], None
    else:
        y_prev, z_prev, gate, w_out_all, ln_g_all, ln_b_all, prev_layer, alpha = prev
        lay = lambda i, t: (prev_layer, 0, 0)
        args = [y_prev, z_prev, x, gate, w_out_all, ln_g_all, ln_b_all]
        in_specs = [row_spec, row_spec, pl.BlockSpec((1, tm, d), row), pl.BlockSpec((1, 1, d), mod_map),
                    pl.BlockSpec((None, w, d), lay, **once), pl.BlockSpec((None, 1, d), lay),
                    pl.BlockSpec((None, 1, d), lay)]
        out_shape.append(jax.ShapeDtypeStruct((b, n, d), F32))
        out_specs.append(pl.BlockSpec((1, tm, d), row))
    args += [sc, sh]
    in_specs += [pl.BlockSpec((1, 1, d), mod_map), pl.BlockSpec((1, 1, d), mod_map)]
    if with_qz:
        args.append(w_all); in_specs.append(wspec(0))
        out_shape.append(row_shape); out_specs.append(row_spec)
    args += [w_all, wv_t if v_transposed else w_all]
    in_specs += [wspec(1), pl.BlockSpec(wv_t.shape, const, **once) if v_transposed else wspec(2)]
    out_shape.append(row_shape); out_specs.append(row_spec)
    if v_transposed:
        out_shape.append(jax.ShapeDtypeStruct((b, w, n), BF16))
        out_specs.append(pl.BlockSpec((1, w, tm), lambda i, t: (i, 0, t)))
    else:
        out_shape.append(row_shape); out_specs.append(row_spec)
    if with_qz:
        args.append(w_all); in_specs.append(wspec(3))
        out_shape.append(row_shape); out_specs.append(row_spec)
    if rope:
        args += list(rope_tables)
        in_specs += [pl.BlockSpec((tm, LANES), lambda i, t: (t, 0))] * 2

    outs = pl.pallas_call(
        functools.partial(_inproj_kernel, rope=rope, v_transposed=v_transposed,
                          with_qz=with_qz, q_scale=q_scale, alpha=alpha),
        grid=(b, n // tm),
        in_specs=in_specs,
        out_specs=out_specs,
        out_shape=out_shape,
        compiler_params=_cparams(("parallel", "parallel")),
        name="inproj",
    )(*args)
    outs = list(outs)
    x_new = [outs.pop(0)] if prev is not None else []
    if with_qz:
        return (*x_new, *outs)
    return (*x_new, None, outs[0], outs[1], None)


def _stack_heads(q, first_head):
    zero = jnp.zeros_like(q)
    return jnp.concatenate([jnp.where(first_head, q, zero), jnp.where(first_head, zero, q)], axis=0)


_NT = (((1,), (1,)), ((), ()))


def _na_kernel(q_ref, k_ref, v_ref, kc_ref, vc_ref, bias_ref, o_ref, s_ref, *, rows):
    kc = kc_ref[0]
    vc = vc_ref[0]
    first_head = lax.broadcasted_iota(jnp.int32, (GRID_W, LANES), 1) < HEAD_DIM
    band = NA_BAND_ROWS * GRID_W
    group = s_ref.shape[1] // (2 * GRID_W)
    n_groups = rows // group
    assert rows % group == 0 and n_groups % 2 == 0 and n_groups >= 4

    def band_start(r):
        r0 = jnp.clip(r - NA_BAND_ROWS // 2, 0, rows - NA_BAND_ROWS)
        return r0, pl.multiple_of(r0 * GRID_W, GRID_W)

    def stage_a(g, slot):
        for t in range(group):
            r = g * group + t
            r0, start = band_start(r)
            dr = r0 - r + NA_BAND_ROWS - 1
            q_rows = q_ref[0, pl.ds(pl.multiple_of(r * GRID_W, GRID_W), GRID_W), :]
            sq = _stack_heads(q_rows, first_head)
            s_loc = lax.dot_general(sq, k_ref[0, pl.ds(start, band), :], _NT, preferred_element_type=F32)
            bias = jnp.concatenate([bias_ref[0, dr + 2 * p] for p in range(NA_BAND_ROWS // 2)], axis=1)
            s_loc = s_loc + bias
            s_ctx = lax.dot_general(sq, kc, _NT, preferred_element_type=F32)
            m = jnp.maximum(jnp.max(s_loc, axis=1, keepdims=True), jnp.max(s_ctx, axis=1, keepdims=True))
            lo = t * 2 * GRID_W
            s_ref[slot, lo:lo + 2 * GRID_W, 0:band] = s_loc - m
            s_ref[slot, lo:lo + 2 * GRID_W, band:] = s_ctx - m

    def stage_b(g, slot):
        for t in range(group):
            r = g * group + t
            _, start = band_start(r)
            lo = t * 2 * GRID_W
            p = jnp.exp2(s_ref[slot, lo:lo + 2 * GRID_W, :])
            denom = jnp.sum(p, axis=1, keepdims=True)
            pb = p.astype(BF16)
            o2 = (jnp.dot(pb[:, :band], v_ref[0, pl.ds(start, band), :], preferred_element_type=F32)
                  + jnp.dot(pb[:, band:], vc, preferred_element_type=F32)) / denom
            y = jnp.where(first_head, o2[:GRID_W], o2[GRID_W:])
            o_ref[0, pl.ds(pl.multiple_of(r * GRID_W, GRID_W), GRID_W), :] = y.astype(o_ref.dtype)

    stage_a(0, 0)

    def body(i, carry):
        stage_a(2 * i + 1, 1)
        stage_b(2 * i, 0)
        stage_a(2 * i + 2, 0)
        stage_b(2 * i + 1, 1)
        return carry

    lax.fori_loop(0, n_groups // 2 - 1, body, 0)
    stage_a(n_groups - 1, 1)
    stage_b(n_groups - 2, 0)
    stage_b(n_groups - 1, 1)


def _na_attention(q, k, v, kc, vc, bias):
    b, n, w = q.shape
    c = kc.shape[1]
    rows = n // GRID_W
    hp = w // LANES
    tok = lambda i, j: (i, 0, j)
    return pl.pallas_call(
        functools.partial(_na_kernel, rows=rows),
        grid=(b, hp),
        in_specs=[
            pl.BlockSpec((1, n, LANES), tok), pl.BlockSpec((1, n, LANES), tok),
            pl.BlockSpec((1, n, LANES), tok),
            pl.BlockSpec((1, c, LANES), tok), pl.BlockSpec((1, c, LANES), tok),
            pl.BlockSpec((1,) + bias.shape[1:], lambda i, j: (j, 0, 0, 0)),
        ],
        out_specs=pl.BlockSpec((1, n, LANES), tok),
        out_shape=jax.ShapeDtypeStruct((b, n, w), BF16),
        scratch_shapes=[pltpu.VMEM((2, NA_GROUP_ROWS * 2 * GRID_W, NA_BAND_ROWS * GRID_W + c), F32)],
        compiler_params=_cparams(("parallel", "parallel")),
        name="na_attention",
    )(q, k, v, kc, vc, bias)


def _na_ctx_kernel(q_ref, kc_ref, vc_ref, o_ref):
    n = q_ref.shape[1]
    first_head = lax.broadcasted_iota(jnp.int32, (n, LANES), 1) < HEAD_DIM
    for hp in range(q_ref.shape[2] // LANES):
        lanes = slice(hp * LANES, (hp + 1) * LANES)
        s = lax.dot_general(_stack_heads(q_ref[0, :, lanes], first_head), kc_ref[0, :, lanes], _NT,
                            preferred_element_type=F32)
        m = jnp.max(s, axis=1, keepdims=True)
        p = jnp.exp2(s - m)
        o2 = (jnp.dot(p.astype(BF16), vc_ref[0, :, lanes], preferred_element_type=F32)
              / jnp.sum(p, axis=1, keepdims=True))
        o_ref[0, :, lanes] = jnp.where(first_head, o2[:n], o2[n:]).astype(o_ref.dtype)


def _na_ctx_attention(qc, kc, vc):
    b, c, w = qc.shape
    tok = pl.BlockSpec((1, c, w), lambda i: (i, 0, 0))
    return pl.pallas_call(
        _na_ctx_kernel,
        grid=(b,),
        in_specs=[tok, tok, tok],
        out_specs=tok,
        out_shape=jax.ShapeDtypeStruct((b, c, w), BF16),
        compiler_params=_cparams(("parallel",)),
        name="na_ctx_attention",
    )(qc, kc, vc)


def _na_bias_table(rpb):
    h = rpb.shape[0]
    wq = jnp.arange(GRID_W)[:, None]
    kj = jnp.arange(GRID_W)[None, :]
    col_start = jnp.clip(wq - NA_WIN_COLS // 2, 0, GRID_W - NA_WIN_COLS)
    valid = (kj >= col_start) & (kj < col_start + NA_WIN_COLS)
    pad = GRID_W - NA_WIN_COLS
    rp = jnp.pad(rpb, ((0, 0), (0, 0), (pad, pad)))
    t = jnp.stack([rp[:, :, GRID_W - 1 - i:2 * GRID_W - 1 - i] for i in range(GRID_W)], axis=2)
    t = jnp.where(valid[None, None], t, NEG)
    t = t.reshape(h // 2, 2, 2 * NA_BAND_ROWS - 1, GRID_W, GRID_W)
    t = jnp.concatenate([t[:, 0], t[:, 1]], axis=2)
    return jnp.concatenate([t[:, :-1], t[:, 1:]], axis=3).astype(F32)


def _diff_kernel(*refs, n_lat, kb, tq, lambda_init):
    it = iter(refs)
    q_ref = next(it)
    k_ref, vt_ref = (next(it), next(it)) if n_lat else (None, None)
    kc_ref, vct_ref, lam_ref, g_ref, o_ref, acc_ref, s_ref = (next(it) for _ in range(7))
    n_ctx = kc_ref.shape[1]
    n_tiles = q_ref.shape[1] // tq
    first_map = lax.broadcasted_iota(jnp.int32, (tq, LANES), 1) < HEAD_DIM
    lp = lam_ref[...]
    lam = (jnp.exp(jnp.sum(lp[0:1] * lp[1:2], axis=1, keepdims=True))
           - jnp.exp(jnp.sum(lp[2:3] * lp[3:4], axis=1, keepdims=True)) + lambda_init)

    def kblock(j):
        return k_ref[0, pl.ds(pl.multiple_of(j * kb, kb), kb), :]

    def vblock(j):
        return vt_ref[0, :, pl.ds(pl.multiple_of(j * kb, kb), kb)]

    for t in range(n_tiles):
        rows = slice(t * tq, (t + 1) * tq)
        qcat = _stack_heads(q_ref[0, rows, :], first_map)
        acc_ref[t] = jnp.zeros(acc_ref.shape[1:], F32)

        def scores(kblk, slot, t=t, qcat=qcat):
            st = lax.dot_general(kblk, qcat, _NT, preferred_element_type=F32)
            s_ref[DIFF_SLOTS * t + slot, 0:kblk.shape[0], :] = st
            return jnp.max(st, axis=0, keepdims=True)

        def consume(slot, vtblk, bmax, m_old, t=t):
            m_new = jnp.maximum(m_old, bmax)
            alpha = jnp.exp2(m_old - m_new)
            p = jnp.exp2(s_ref[DIFF_SLOTS * t + slot, 0:vtblk.shape[1], :] - m_new)
            vaug = jnp.concatenate([vtblk, jnp.ones((SUBLANES_BF16, vtblk.shape[1]), BF16)], axis=0)
            acc_ref[t] = alpha * acc_ref[t] + jnp.dot(vaug, p.astype(BF16), preferred_element_type=F32)
            return m_new

        m = jnp.full((1, 2 * tq), NEG, F32)
        if n_lat:
            nb = n_lat // kb
            assert nb >= 2 and (nb - 2) % DIFF_SLOTS == 0 and n_ctx <= kb
            bm = scores(kblock(0), 0)

            def body(i, carry, scores=scores, consume=consume):
                m, bm = carry
                for u in range(DIFF_SLOTS):
                    j = DIFF_SLOTS * i + u
                    bm_next = scores(kblock(j + 1), (u + 1) % DIFF_SLOTS)
                    m = consume(u, vblock(j), bm, m)
                    bm = bm_next
                return m, bm

            m, bm = lax.fori_loop(0, (nb - 2) // DIFF_SLOTS, body, (m, bm))
            bm_next = scores(kblock(nb - 1), (nb - 1) % DIFF_SLOTS)
            m = consume((nb - 2) % DIFF_SLOTS, vblock(nb - 2), bm, m)
            bm_ctx = scores(kc_ref[0], nb % DIFF_SLOTS)
            m = consume((nb - 1) % DIFF_SLOTS, vblock(nb - 1), bm_next, m)
            consume(nb % DIFF_SLOTS, vct_ref[0], bm_ctx, m)
        else:
            consume(0, vct_ref[0], scores(kc_ref[0], 0), m)

        o_all = acc_ref[t, 0:LANES, :] / acc_ref[t, LANES:LANES + 1, :]
        ot = o_all[:, :tq] - lam * o_all[:, tq:]
        ms = jnp.mean(ot * ot, axis=0, keepdims=True)
        ot = ot * lax.rsqrt(ms + LN_EPS)
        o_ref[0, rows, :] = (ot.T * g_ref[...] * (1.0 - lambda_init)).astype(o_ref.dtype)


def _diff_attention(q, k, vt, kc, vct, lam_params, subln_g, *, lambda_init, tq, kb, tiles=1):
    b, nq, w = q.shape
    c = kc.shape[1]
    heads = w // LANES
    n_lat = 0 if k is None else k.shape[1]
    step_rows = tiles * tq
    args = [q]
    in_specs = [pl.BlockSpec((1, step_rows, LANES), lambda i, h, t: (i, t, h))]
    if n_lat:
        args += [k, vt]
        in_specs += [pl.BlockSpec((1, n_lat, LANES), lambda i, h, t: (i, 0, h)),
                     pl.BlockSpec((1, LANES, n_lat), lambda i, h, t: (i, h, 0))]
    args += [kc, vct, lam_params, subln_g.reshape(1, LANES)]
    in_specs += [pl.BlockSpec((1, c, LANES), lambda i, h, t: (i, 0, h)),
                 pl.BlockSpec((1, LANES, c), lambda i, h, t: (i, h, 0)),
                 pl.BlockSpec(lam_params.shape, lambda i, h, t: (0, 0)),
                 pl.BlockSpec((1, LANES), lambda i, h, t: (0, 0))]
    return pl.pallas_call(
        functools.partial(_diff_kernel, n_lat=n_lat, kb=kb, tq=tq, lambda_init=lambda_init),
        grid=(b, heads, nq // step_rows),
        in_specs=in_specs,
        out_specs=pl.BlockSpec((1, step_rows, LANES), lambda i, h, t: (i, t, h)),
        out_shape=jax.ShapeDtypeStruct((b, nq, w), BF16),
        scratch_shapes=[pltpu.VMEM((tiles, LANES + SUBLANES_BF16, 2 * tq), F32),
                        pltpu.VMEM((DIFF_SLOTS * tiles, kb, 2 * tq), F32)],
        compiler_params=_cparams(("parallel", "parallel", "arbitrary")),
        name="diff_attention",
    )(*args)


def _outproj_kernel(y_ref, z_ref, x_ref, g_ref, w_ref, lng_ref, lnb_ref, o_ref, *, alpha):
    o_ref[0] = _outproj_value(y_ref, z_ref, x_ref, g_ref, w_ref, lng_ref, lnb_ref, alpha)


def _outproj(y, z, x, gate, w_out_all, ln_g_all, ln_b_all, layer, *, tm, alpha):
    b, n, d = x.shape
    w = y.shape[2]
    per_batch = gate.shape[0] == b
    mod_map = (lambda i, t: (i, 0, 0)) if per_batch else (lambda i, t: (0, 0, 0))
    row = lambda i, t: (i, t, 0)
    lay = lambda i, t: (layer, 0, 0)
    return pl.pallas_call(
        functools.partial(_outproj_kernel, alpha=alpha),
        grid=(b, n // tm),
        in_specs=[
            pl.BlockSpec((1, tm, w), row), pl.BlockSpec((1, tm, w), row),
            pl.BlockSpec((1, tm, d), row), pl.BlockSpec((1, 1, d), mod_map),
            pl.BlockSpec((None, w, d), lay), pl.BlockSpec((None, 1, d), lay), pl.BlockSpec((None, 1, d), lay),
        ],
        out_specs=pl.BlockSpec((1, tm, d), row),
        out_shape=jax.ShapeDtypeStruct((b, n, d), F32),
        compiler_params=_cparams(("parallel", "parallel")),
        name="outproj_ln",
    )(y, z, x, gate, w_out_all, ln_g_all, ln_b_all)


def _rope_tables(n_tok):
    t = jnp.arange(n_tok, dtype=jnp.int32)
    row = (t // GRID_W).astype(F32)[:, None]
    col = (t % GRID_W).astype(F32)[:, None]
    n_freq = HEAD_DIM // 4
    inv_freq = ROPE_BASE ** (-jnp.arange(n_freq, dtype=F32) / n_freq)
    lane = jnp.arange(LANES)
    pos = jnp.where(((lane % HEAD_DIM) // (HEAD_DIM // 2) == 0)[None, :], row, col)
    ang = pos * inv_freq[lane % n_freq][None, :]
    first = ((lane // n_freq) % 2 == 0)[None, :]
    return jnp.cos(ang), jnp.where(first, -jnp.sin(ang), jnp.sin(ang))


def kernel(x, c, ctx, c_ctx, w_mod, b_mod, w_in, w_out, ln_g, ln_b, na_rpb, diff_lambda, diff_subln_g):
    depth, d, _ = w_mod.shape
    b, n, _ = x.shape
    w = w_out.shape[1]
    n_ctx = ctx.shape[1]
    alpha = (2.0 * depth) ** 0.25
    q_scale = HEAD_DIM ** -0.5 * LOG2E

    cond = jnp.concatenate([c, c_ctx[None, :], jnp.zeros((16 - b - 1, d), F32)], axis=0)
    mods = _modulation(cond, w_mod, b_mod)
    w_in_bf = w_in.astype(BF16)
    w_out_bf = w_out.astype(BF16)
    ln_g3, ln_b3 = ln_g[:, None, :], ln_b[:, None, :]
    rope_tables = _rope_tables(n)
    wv_t_all = jnp.swapaxes(w_in[1::2, :, 2 * w:3 * w], 1, 2).astype(BF16)

    prev = prev_c = None
    for l in range(depth):
        need_ctx = l < depth - 1
        is_diff = l % 2 == 1
        m = mods[l]
        sh, sc, g = (m[:b, None, i * d:(i + 1) * d] for i in range(3))
        shc, scc, gc = (m[b:b + 1, None, i * d:(i + 1) * d] for i in range(3))
        wv_t = wv_t_all[l // 2] if is_diff else None
        res = _inproj(x, sc, sh, w_in_bf, l, wv_t, tm=512, rope_tables=rope_tables if is_diff else None,
                      v_transposed=is_diff, with_qz=True, q_scale=q_scale, prev=prev)
        if prev is not None:
            x, res = res[0], res[1:]
        q, k, v, z = res
        res = _inproj(ctx, scc, shc, w_in_bf, l, wv_t, tm=n_ctx, rope_tables=None,
                      v_transposed=is_diff, with_qz=need_ctx, q_scale=q_scale, prev=prev_c)
        if prev_c is not None:
            ctx, res = res[0], res[1:]
        qc, kc, vc, zc = res
        if is_diff:
            lam_p, sub_g = diff_lambda[l // 2], diff_subln_g[l // 2]
            lambda_init = 0.8 - 0.6 * math.exp(-0.3 * l)
            y = _diff_attention(q, k, v, kc, vc, lam_p, sub_g, lambda_init=lambda_init, tq=512, kb=512, tiles=4)
            if need_ctx:
                yc = _diff_attention(qc, None, None, kc, vc, lam_p, sub_g,
                                     lambda_init=lambda_init, tq=n_ctx, kb=512)
        else:
            y = _na_attention(q, k, v, kc, vc, _na_bias_table(na_rpb[l // 2] * LOG2E))
            if need_ctx:
                yc = _na_ctx_attention(qc, kc, vc)
        prev = (y, z, g, w_out_bf, ln_g3, ln_b3, l, alpha)
        prev_c = (yc, zc, gc, w_out_bf, ln_g3, ln_b3, l, alpha) if need_ctx else None
    y, z, g, w_out_all, ln_g_all, ln_b_all, last = prev[:7]
    return _outproj(y, z, x, g, w_out_all, ln_g_all, ln_b_all, last, tm=512, alpha=alpha)
```

```python
import functools
import math

import jax
import jax.numpy as jnp
from jax import lax
from jax.experimental import pallas as pl
from jax.experimental.pallas import tpu as pltpu

F32 = jnp.float32
BF16 = jnp.bfloat16

GRID_W = 64
ROPE_BASE = 10000.0
LN_EPS = 1e-5
NEG = -1e30
LANES = 128
HEAD_DIM = 64
NA_BAND_ROWS = 8
NA_WIN_COLS = 16
NA_PAIR_BAND = 10
SUBLANES_BF16 = 16
DIFF_SLOTS = 3
LOG2E = math.log2(math.e)
VMEM_LIMIT = 48 * 1024 * 1024


def _cparams(sem):
    return pltpu.CompilerParams(dimension_semantics=sem, vmem_limit_bytes=VMEM_LIMIT)


def _mod_kernel(c_ref, w_ref, b_ref, o_ref):
    c = c_ref[...]
    s = c * jax.nn.sigmoid(c)
    o_ref[0] = jnp.dot(s, w_ref[0], preferred_element_type=F32) + b_ref[0]


def _modulation(cond, w_mod, b_mod):
    depth, d, d3 = w_mod.shape
    rows = cond.shape[0]
    tn = 1024
    return pl.pallas_call(
        _mod_kernel,
        grid=(depth, d3 // tn),
        in_specs=[
            pl.BlockSpec((rows, d), lambda l, j: (0, 0)),
            pl.BlockSpec((1, d, tn), lambda l, j: (l, 0, j)),
            pl.BlockSpec((1, 1, tn), lambda l, j: (l, 0, j)),
        ],
        out_specs=pl.BlockSpec((1, rows, tn), lambda l, j: (l, 0, j)),
        out_shape=jax.ShapeDtypeStruct((depth, rows, d3), F32),
        compiler_params=_cparams(("parallel", "parallel")),
        name="modulation",
    )(cond, w_mod, b_mod.reshape(depth, 1, d3))


def _rope_store(t, cos, sin, is_first, o_ref):
    for cidx in range(t.shape[1] // LANES):
        tc = t[:, cidx * LANES:(cidx + 1) * LANES]
        partner = jnp.where(is_first, pltpu.roll(tc, LANES - 16, axis=1), pltpu.roll(tc, 16, axis=1))
        o_ref[0, :, cidx * LANES:(cidx + 1) * LANES] = (tc * cos + partner * sin).astype(o_ref.dtype)


def _outproj_value(y_ref, z_ref, x_ref, g_ref, w_ref, lng_ref, lnb_ref, alpha):
    z = z_ref[0].astype(F32)
    u = (y_ref[0].astype(F32) * (z * jax.nn.sigmoid(z))).astype(BF16)
    out = jnp.dot(u, w_ref[...], preferred_element_type=F32)
    r = alpha * x_ref[0] + (1.0 + g_ref[0]) * out
    mu = jnp.mean(r, axis=1, keepdims=True)
    rc = r - mu
    var = jnp.mean(rc * rc, axis=1, keepdims=True)
    return rc * lax.rsqrt(var + LN_EPS) * lng_ref[...] + lnb_ref[...]


def _inproj_kernel(*refs, rope, with_qz, q_scale, alpha):
    it = iter(refs)
    prev = [next(it) for _ in range(7)] if alpha is not None else None
    x_ref = prev[2] if prev else next(it)
    sc_ref, sh_ref = next(it), next(it)
    wq_ref = next(it) if with_qz else None
    wk_ref, wv_ref = next(it), next(it)
    wz_ref = next(it) if with_qz else None
    cos_ref, sin_ref = (next(it), next(it)) if rope else (None, None)
    xo_ref = next(it) if prev else None
    q_ref = next(it) if with_qz else None
    k_ref, v_ref = next(it), next(it)
    z_ref = next(it) if with_qz else None

    if prev:
        x_val = _outproj_value(*prev, alpha)
        xo_ref[0] = x_val
    else:
        x_val = x_ref[0]
    h = (x_val * (1.0 + sc_ref[0]) + sh_ref[0]).astype(BF16)
    if rope:
        cos, sin = cos_ref[...], sin_ref[...]
        lane = lax.broadcasted_iota(jnp.int32, cos.shape, 1)
        is_first = (lane // 16) % 2 == 0
    if with_qz:
        q = jnp.dot(h, wq_ref[...], preferred_element_type=F32) * q_scale
        if rope:
            _rope_store(q, cos, sin, is_first, q_ref)
        else:
            q_ref[0] = q.astype(q_ref.dtype)
    k = jnp.dot(h, wk_ref[...], preferred_element_type=F32)
    if rope:
        _rope_store(k, cos, sin, is_first, k_ref)
    else:
        k_ref[0] = k.astype(k_ref.dtype)
    v_ref[0] = lax.dot_general(wv_ref[...], h, (((1,), (1,)), ((), ())),
                               preferred_element_type=F32).astype(v_ref.dtype)
    if with_qz:
        z_ref[0] = jnp.dot(h, wz_ref[...], preferred_element_type=F32).astype(z_ref.dtype)


def _inproj(x, sc, sh, w_all, layer, wv_t, *, tm, rope_tables, with_qz, q_scale, prev=None):
    b, n, d = x.shape
    w = w_all.shape[2] // 4
    per_batch = sc.shape[0] == b
    mod_map = (lambda i, t: (i, 0, 0)) if per_batch else (lambda i, t: (0, 0, 0))
    const = lambda i, t: (0, 0)
    rope = rope_tables is not None
    once = dict(pipeline_mode=pl.Buffered(1))
    wspec = lambda col: pl.BlockSpec((None, d, w), lambda i, t: (layer, 0, col), **once)
    row = lambda i, t: (i, t, 0)
    row_spec = pl.BlockSpec((1, tm, w), row)
    row_shape = jax.ShapeDtypeStruct((b, n, w), BF16)

    out_shape, out_specs = [], []
    if prev is None:
        args, in_specs, alpha = [x], [pl.BlockSpec((1, tm, d), row)], None
    else:
        y_prev, z_prev, gate, w_out_all, ln_g_all, ln_b_all, prev_layer, alpha = prev
        lay = lambda i, t: (prev_layer, 0, 0)
        args = [y_prev, z_prev, x, gate, w_out_all, ln_g_all, ln_b_all]
        in_specs = [row_spec, row_spec, pl.BlockSpec((1, tm, d), row), pl.BlockSpec((1, 1, d), mod_map),
                    pl.BlockSpec((None, w, d), lay, **once), pl.BlockSpec((None, 1, d), lay),
                    pl.BlockSpec((None, 1, d), lay)]
        out_shape.append(jax.ShapeDtypeStruct((b, n, d), F32))
        out_specs.append(pl.BlockSpec((1, tm, d), row))
    args += [sc, sh]
    in_specs += [pl.BlockSpec((1, 1, d), mod_map), pl.BlockSpec((1, 1, d), mod_map)]
    if with_qz:
        args.append(w_all); in_specs.append(wspec(0))
        out_shape.append(row_shape); out_specs.append(row_spec)
    args += [w_all, wv_t]
    in_specs += [wspec(1), pl.BlockSpec(wv_t.shape, const, **once)]
    out_shape.append(row_shape); out_specs.append(row_spec)
    out_shape.append(jax.ShapeDtypeStruct((b, w, n), BF16))
    out_specs.append(pl.BlockSpec((1, w, tm), lambda i, t: (i, 0, t)))
    if with_qz:
        args.append(w_all); in_specs.append(wspec(3))
        out_shape.append(row_shape); out_specs.append(row_spec)
    if rope:
        args += list(rope_tables)
        in_specs += [pl.BlockSpec((tm, LANES), lambda i, t: (t, 0))] * 2

    outs = pl.pallas_call(
        functools.partial(_inproj_kernel, rope=rope, with_qz=with_qz, q_scale=q_scale, alpha=alpha),
        grid=(b, n // tm),
        in_specs=in_specs,
        out_specs=out_specs,
        out_shape=out_shape,
        compiler_params=_cparams(("parallel", "parallel")),
        name="inproj",
    )(*args)
    outs = list(outs)
    x_new = [outs.pop(0)] if prev is not None else []
    if with_qz:
        return (*x_new, *outs)
    return (*x_new, None, outs[0], outs[1], None)


def _stack_heads(q, first_head):
    zero = jnp.zeros_like(q)
    return jnp.concatenate([jnp.where(first_head, q, zero), jnp.where(first_head, zero, q)], axis=0)


_NT = (((1,), (1,)), ((), ()))


def _na_kernel(q_ref, k_ref, vt_ref, kc_ref, vct_ref, bias_ref, o_ref, s_ref, *, rows):
    kc = kc_ref[0]
    n_ctx = kc.shape[0]
    first_head = lax.broadcasted_iota(jnp.int32, (GRID_W, LANES), 1) < HEAD_DIM
    band = NA_PAIR_BAND * GRID_W
    neg_tile = bias_ref.shape[1] - 1
    n_pairs = rows // 2
    assert rows % 2 == 0 and n_pairs % 4 == 0 and n_pairs >= 8

    def band_row(pair):
        return jnp.clip(2 * pair - NA_BAND_ROWS // 2, 0, rows - NA_PAIR_BAND)

    def stage_a(pair, slot):
        r = 2 * pair
        b0 = band_row(pair)
        q2 = q_ref[0, pl.ds(pl.multiple_of(r * GRID_W, 2 * GRID_W), 2 * GRID_W), :]
        qcat = jnp.concatenate([_stack_heads(q2[:GRID_W], first_head),
                                _stack_heads(q2[GRID_W:], first_head)], axis=0)
        kband = k_ref[0, pl.ds(pl.multiple_of(b0 * GRID_W, 2 * GRID_W), band), :]
        tiles = []
        for i in range(NA_PAIR_BAND):
            per_row = []
            for rho in range(2):
                start = jnp.clip(r + rho - NA_BAND_ROWS // 2, 0, rows - NA_BAND_ROWS)
                off = b0 + i - start
                dr = b0 + i - (r + rho) + NA_BAND_ROWS - 1
                idx = jnp.where((off >= 0) & (off < NA_BAND_ROWS), dr, neg_tile)
                per_row.append(bias_ref[0, idx])
            tiles.append(jnp.concatenate(per_row, axis=1))
        st = lax.dot_general(kband, qcat, _NT, preferred_element_type=F32) + jnp.concatenate(tiles, axis=0)
        sc = lax.dot_general(kc, qcat, _NT, preferred_element_type=F32)
        s_ref[slot, 0:band, :] = st
        s_ref[slot, band:, :] = sc
        return jnp.maximum(jnp.max(st, axis=0, keepdims=True), jnp.max(sc, axis=0, keepdims=True))

    def stage_b(pair, slot, m):
        r = 2 * pair
        b0 = band_row(pair)
        p = jnp.exp2(s_ref[slot] - m).astype(BF16)
        vband = vt_ref[0, :, pl.ds(pl.multiple_of(b0 * GRID_W, 2 * GRID_W), band)]
        vaug = jnp.concatenate([vband, vct_ref[0]], axis=1)
        vaug = jnp.concatenate([vaug, jnp.ones((SUBLANES_BF16, band + n_ctx), BF16)], axis=0)
        acc = jnp.dot(vaug, p, preferred_element_type=F32)
        ot = (acc[0:LANES] / acc[LANES:LANES + 1]).T
        y = jnp.concatenate(
            [jnp.where(first_head, ot[rho * LANES:rho * LANES + GRID_W], ot[rho * LANES + GRID_W:(rho + 1) * LANES])
             for rho in range(2)], axis=0)
        o_ref[0, pl.ds(pl.multiple_of(r * GRID_W, 2 * GRID_W), 2 * GRID_W), :] = y.astype(o_ref.dtype)

    m0 = stage_a(0, 0)
    m1 = stage_a(1, 1)

    def body(i, carry):
        m0, m1 = carry
        j = 4 * i
        stage_b(j, 0, m0)
        m2 = stage_a(j + 2, 2)
        stage_b(j + 1, 1, m1)
        m3 = stage_a(j + 3, 3)
        stage_b(j + 2, 2, m2)
        m0 = stage_a(j + 4, 0)
        stage_b(j + 3, 3, m3)
        m1 = stage_a(j + 5, 1)
        return m0, m1

    m0, m1 = lax.fori_loop(0, n_pairs // 4 - 1, body, (m0, m1))
    j = n_pairs - 4
    stage_b(j, 0, m0)
    m2 = stage_a(j + 2, 2)
    stage_b(j + 1, 1, m1)
    m3 = stage_a(j + 3, 3)
    stage_b(j + 2, 2, m2)
    stage_b(j + 3, 3, m3)


def _na_attention(q, k, vt, kc, vct, bias):
    b, n, w = q.shape
    c = kc.shape[1]
    rows = n // GRID_W
    hp = w // LANES
    tok = lambda i, j: (i, 0, j)
    chan = lambda i, j: (i, j, 0)
    return pl.pallas_call(
        functools.partial(_na_kernel, rows=rows),
        grid=(b, hp),
        in_specs=[
            pl.BlockSpec((1, n, LANES), tok), pl.BlockSpec((1, n, LANES), tok),
            pl.BlockSpec((1, LANES, n), chan),
            pl.BlockSpec((1, c, LANES), tok), pl.BlockSpec((1, LANES, c), chan),
            pl.BlockSpec((1,) + bias.shape[1:], lambda i, j: (j, 0, 0, 0)),
        ],
        out_specs=pl.BlockSpec((1, n, LANES), tok),
        out_shape=jax.ShapeDtypeStruct((b, n, w), BF16),
        scratch_shapes=[pltpu.VMEM((4, NA_PAIR_BAND * GRID_W + c, 2 * LANES), F32)],
        compiler_params=_cparams(("parallel", "parallel")),
        name="na_attention",
    )(q, k, vt, kc, vct, bias)


def _na_ctx_kernel(q_ref, kc_ref, vct_ref, o_ref):
    n = q_ref.shape[1]
    first_head = lax.broadcasted_iota(jnp.int32, (n, LANES), 1) < HEAD_DIM
    for hp in range(q_ref.shape[2] // LANES):
        lanes = slice(hp * LANES, (hp + 1) * LANES)
        s = lax.dot_general(_stack_heads(q_ref[0, :, lanes], first_head), kc_ref[0, :, lanes], _NT,
                            preferred_element_type=F32)
        m = jnp.max(s, axis=1, keepdims=True)
        p = jnp.exp2(s - m)
        o2 = (lax.dot_general(p.astype(BF16), vct_ref[0, lanes, :], _NT, preferred_element_type=F32)
              / jnp.sum(p, axis=1, keepdims=True))
        o_ref[0, :, lanes] = jnp.where(first_head, o2[:n], o2[n:]).astype(o_ref.dtype)


def _na_ctx_attention(qc, kc, vct):
    b, c, w = qc.shape
    tok = pl.BlockSpec((1, c, w), lambda i: (i, 0, 0))
    return pl.pallas_call(
        _na_ctx_kernel,
        grid=(b,),
        in_specs=[tok, tok, pl.BlockSpec((1, w, c), lambda i: (i, 0, 0))],
        out_specs=tok,
        out_shape=jax.ShapeDtypeStruct((b, c, w), BF16),
        compiler_params=_cparams(("parallel",)),
        name="na_ctx_attention",
    )(qc, kc, vct)


def _na_bias_table(rpb):
    h = rpb.shape[0]
    wq = jnp.arange(GRID_W)[:, None]
    kj = jnp.arange(GRID_W)[None, :]
    col_start = jnp.clip(wq - NA_WIN_COLS // 2, 0, GRID_W - NA_WIN_COLS)
    valid = (kj >= col_start) & (kj < col_start + NA_WIN_COLS)
    pad = GRID_W - NA_WIN_COLS
    rp = jnp.pad(rpb, ((0, 0), (0, 0), (pad, pad)))
    t = jnp.stack([rp[:, :, GRID_W - 1 - i:2 * GRID_W - 1 - i] for i in range(GRID_W)], axis=2)
    t = jnp.where(valid[None, None], t, NEG)
    t = t.reshape(h // 2, 2, 2 * NA_BAND_ROWS - 1, GRID_W, GRID_W)
    t = t.transpose(0, 2, 4, 1, 3).reshape(h // 2, 2 * NA_BAND_ROWS - 1, GRID_W, 2 * GRID_W)
    neg = jnp.full((h // 2, 1, GRID_W, 2 * GRID_W), NEG, F32)
    return jnp.concatenate([t.astype(F32), neg], axis=1)


def _diff_kernel(*refs, n_lat, kb, tq, lambda_init):
    it = iter(refs)
    q_ref = next(it)
    k_ref, vt_ref = (next(it), next(it)) if n_lat else (None, None)
    kc_ref, vct_ref, lam_ref, g_ref, o_ref, acc_ref, s_ref = (next(it) for _ in range(7))
    n_ctx = kc_ref.shape[1]
    n_tiles = q_ref.shape[1] // tq
    first_map = lax.broadcasted_iota(jnp.int32, (tq, LANES), 1) < HEAD_DIM
    lp = lam_ref[...]
    lam = (jnp.exp(jnp.sum(lp[0:1] * lp[1:2], axis=1, keepdims=True))
           - jnp.exp(jnp.sum(lp[2:3] * lp[3:4], axis=1, keepdims=True)) + lambda_init)

    def kblock(j):
        return k_ref[0, pl.ds(pl.multiple_of(j * kb, kb), kb), :]

    def vblock(j):
        return vt_ref[0, :, pl.ds(pl.multiple_of(j * kb, kb), kb)]

    for t in range(n_tiles):
        rows = slice(t * tq, (t + 1) * tq)
        qcat = _stack_heads(q_ref[0, rows, :], first_map)
        acc_ref[t] = jnp.zeros(acc_ref.shape[1:], F32)

        def scores(kblk, slot, t=t, qcat=qcat):
            st = lax.dot_general(kblk, qcat, _NT, preferred_element_type=F32)
            s_ref[DIFF_SLOTS * t + slot, 0:kblk.shape[0], :] = st
            return jnp.max(st, axis=0, keepdims=True)

        def consume(slot, vtblk, bmax, m_old, t=t):
            m_new = jnp.maximum(m_old, bmax)
            alpha = jnp.exp2(m_old - m_new)
            p = jnp.exp2(s_ref[DIFF_SLOTS * t + slot, 0:vtblk.shape[1], :] - m_new)
            vaug = jnp.concatenate([vtblk, jnp.ones((SUBLANES_BF16, vtblk.shape[1]), BF16)], axis=0)
            acc_ref[t] = alpha * acc_ref[t] + jnp.dot(vaug, p.astype(BF16), preferred_element_type=F32)
            return m_new

        m = jnp.full((1, 2 * tq), NEG, F32)
        if n_lat:
            nb = n_lat // kb
            assert nb >= 2 and (nb - 2) % DIFF_SLOTS == 0 and n_ctx <= kb
            bm = scores(kblock(0), 0)

            def body(i, carry, scores=scores, consume=consume):
                m, bm = carry
                for u in range(DIFF_SLOTS):
                    j = DIFF_SLOTS * i + u
                    bm_next = scores(kblock(j + 1), (u + 1) % DIFF_SLOTS)
                    m = consume(u, vblock(j), bm, m)
                    bm = bm_next
                return m, bm

            m, bm = lax.fori_loop(0, (nb - 2) // DIFF_SLOTS, body, (m, bm))
            bm_next = scores(kblock(nb - 1), (nb - 1) % DIFF_SLOTS)
            m = consume((nb - 2) % DIFF_SLOTS, vblock(nb - 2), bm, m)
            bm_ctx = scores(kc_ref[0], nb % DIFF_SLOTS)
            m = consume((nb - 1) % DIFF_SLOTS, vblock(nb - 1), bm_next, m)
            consume(nb % DIFF_SLOTS, vct_ref[0], bm_ctx, m)
        else:
            consume(0, vct_ref[0], scores(kc_ref[0], 0), m)

        o_all = acc_ref[t, 0:LANES, :] / acc_ref[t, LANES:LANES + 1, :]
        ot = o_all[:, :tq] - lam * o_all[:, tq:]
        ms = jnp.mean(ot * ot, axis=0, keepdims=True)
        ot = ot * lax.rsqrt(ms + LN_EPS)
        o_ref[0, rows, :] = (ot.T * g_ref[...] * (1.0 - lambda_init)).astype(o_ref.dtype)


def _diff_attention(q, k, vt, kc, vct, lam_params, subln_g, *, lambda_init, tq, kb, tiles=1):
    b, nq, w = q.shape
    c = kc.shape[1]
    heads = w // LANES
    n_lat = 0 if k is None else k.shape[1]
    step_rows = tiles * tq
    args = [q]
    in_specs = [pl.BlockSpec((1, step_rows, LANES), lambda i, h, t: (i, t, h))]
    if n_lat:
        args += [k, vt]
        in_specs += [pl.BlockSpec((1, n_lat, LANES), lambda i, h, t: (i, 0, h)),
                     pl.BlockSpec((1, LANES, n_lat), lambda i, h, t: (i, h, 0))]
    args += [kc, vct, lam_params, subln_g.reshape(1, LANES)]
    in_specs += [pl.BlockSpec((1, c, LANES), lambda i, h, t: (i, 0, h)),
                 pl.BlockSpec((1, LANES, c), lambda i, h, t: (i, h, 0)),
                 pl.BlockSpec(lam_params.shape, lambda i, h, t: (0, 0)),
                 pl.BlockSpec((1, LANES), lambda i, h, t: (0, 0))]
    return pl.pallas_call(
        functools.partial(_diff_kernel, n_lat=n_lat, kb=kb, tq=tq, lambda_init=lambda_init),
        grid=(b, heads, nq // step_rows),
        in_specs=in_specs,
        out_specs=pl.BlockSpec((1, step_rows, LANES), lambda i, h, t: (i, t, h)),
        out_shape=jax.ShapeDtypeStruct((b, nq, w), BF16),
        scratch_shapes=[pltpu.VMEM((tiles, LANES + SUBLANES_BF16, 2 * tq), F32),
                        pltpu.VMEM((DIFF_SLOTS * tiles, kb, 2 * tq), F32)],
        compiler_params=_cparams(("parallel", "parallel", "arbitrary")),
        name="diff_attention",
    )(*args)


def _outproj_kernel(y_ref, z_ref, x_ref, g_ref, w_ref, lng_ref, lnb_ref, o_ref, *, alpha):
    o_ref[0] = _outproj_value(y_ref, z_ref, x_ref, g_ref, w_ref, lng_ref, lnb_ref, alpha)


def _outproj(y, z, x, gate, w_out_all, ln_g_all, ln_b_all, layer, *, tm, alpha):
    b, n, d = x.shape
    w = y.shape[2]
    per_batch = gate.shape[0] == b
    mod_map = (lambda i, t: (i, 0, 0)) if per_batch else (lambda i, t: (0, 0, 0))
    row = lambda i, t: (i, t, 0)
    lay = lambda i, t: (layer, 0, 0)
    return pl.pallas_call(
        functools.partial(_outproj_kernel, alpha=alpha),
        grid=(b, n // tm),
        in_specs=[
            pl.BlockSpec((1, tm, w), row), pl.BlockSpec((1, tm, w), row),
            pl.BlockSpec((1, tm, d), row), pl.BlockSpec((1, 1, d), mod_map),
            pl.BlockSpec((None, w, d), lay), pl.BlockSpec((None, 1, d), lay), pl.BlockSpec((None, 1, d), lay),
        ],
        out_specs=pl.BlockSpec((1, tm, d), row),
        out_shape=jax.ShapeDtypeStruct((b, n, d), F32),
        compiler_params=_cparams(("parallel", "parallel")),
        name="outproj_ln",
    )(y, z, x, gate, w_out_all, ln_g_all, ln_b_all)


def _rope_tables(n_tok):
    t = jnp.arange(n_tok, dtype=jnp.int32)
    row = (t // GRID_W).astype(F32)[:, None]
    col = (t % GRID_W).astype(F32)[:, None]
    n_freq = HEAD_DIM // 4
    inv_freq = ROPE_BASE ** (-jnp.arange(n_freq, dtype=F32) / n_freq)
    lane = jnp.arange(LANES)
    pos = jnp.where(((lane % HEAD_DIM) // (HEAD_DIM // 2) == 0)[None, :], row, col)
    ang = pos * inv_freq[lane % n_freq][None, :]
    first = ((lane // n_freq) % 2 == 0)[None, :]
    return jnp.cos(ang), jnp.where(first, -jnp.sin(ang), jnp.sin(ang))


def kernel(x, c, ctx, c_ctx, w_mod, b_mod, w_in, w_out, ln_g, ln_b, na_rpb, diff_lambda, diff_subln_g):
    depth, d, _ = w_mod.shape
    b, n, _ = x.shape
    w = w_out.shape[1]
    n_ctx = ctx.shape[1]
    alpha = (2.0 * depth) ** 0.25
    q_scale = HEAD_DIM ** -0.5 * LOG2E

    cond = jnp.concatenate([c, c_ctx[None, :], jnp.zeros((16 - b - 1, d), F32)], axis=0)
    mods = _modulation(cond, w_mod, b_mod)
    w_in_bf = w_in.astype(BF16)
    w_out_bf = w_out.astype(BF16)
    ln_g3, ln_b3 = ln_g[:, None, :], ln_b[:, None, :]
    rope_tables = _rope_tables(n)
    wv_t_all = jnp.swapaxes(w_in[:, :, 2 * w:3 * w], 1, 2).astype(BF16)

    prev = prev_c = None
    for l in range(depth):
        need_ctx = l < depth - 1
        is_diff = l % 2 == 1
        m = mods[l]
        sh, sc, g = (m[:b, None, i * d:(i + 1) * d] for i in range(3))
        shc, scc, gc = (m[b:b + 1, None, i * d:(i + 1) * d] for i in range(3))
        wv_t = wv_t_all[l]
        res = _inproj(x, sc, sh, w_in_bf, l, wv_t, tm=512, rope_tables=rope_tables if is_diff else None,
                      with_qz=True, q_scale=q_scale, prev=prev)
        if prev is not None:
            x, res = res[0], res[1:]
        q, k, v, z = res
        res = _inproj(ctx, scc, shc, w_in_bf, l, wv_t, tm=n_ctx, rope_tables=None,
                      with_qz=need_ctx, q_scale=q_scale, prev=prev_c)
        if prev_c is not None:
            ctx, res = res[0], res[1:]
        qc, kc, vc, zc = res
        if is_diff:
            lam_p, sub_g = diff_lambda[l // 2], diff_subln_g[l // 2]
            lambda_init = 0.8 - 0.6 * math.exp(-0.3 * l)
            y = _diff_attention(q, k, v, kc, vc, lam_p, sub_g, lambda_init=lambda_init, tq=512, kb=512, tiles=4)
            if need_ctx:
                yc = _diff_attention(qc, None, None, kc, vc, lam_p, sub_g,
                                     lambda_init=lambda_init, tq=n_ctx, kb=512)
        else:
            y = _na_attention(q, k, v, kc, vc, _na_bias_table(na_rpb[l // 2] * LOG2E))
            if need_ctx:
                yc = _na_ctx_attention(qc, kc, vc)
        prev = (y, z, g, w_out_bf, ln_g3, ln_b3, l, alpha)
        prev_c = (yc, zc, gc, w_out_bf, ln_g3, ln_b3, l, alpha) if need_ctx else None
    y, z, g, w_out_all, ln_g_all, ln_b_all, last = prev[:7]
    return _outproj(y, z, x, g, w_out_all, ln_g_all, ln_b_all, last, tm=512, alpha=alpha)
```

```python
import functools
import math

import jax
import jax.numpy as jnp
from jax import lax
from jax.experimental import pallas as pl
from jax.experimental.pallas import tpu as pltpu

F32 = jnp.float32
BF16 = jnp.bfloat16

GRID_W = 64
ROPE_BASE = 10000.0
LN_EPS = 1e-5
NEG = -1e30
LANES = 128
HEAD_DIM = 64
NA_BAND_ROWS = 8
NA_WIN_COLS = 16
NA_PAIR_BAND = 10
SUBLANES_BF16 = 16
DIFF_SLOTS = 3
LOG2E = math.log2(math.e)
VMEM_LIMIT = 48 * 1024 * 1024


def _cparams(sem):
    return pltpu.CompilerParams(dimension_semantics=sem, vmem_limit_bytes=VMEM_LIMIT)


def _mod_kernel(c_ref, w_ref, b_ref, o_ref):
    c = c_ref[...]
    s = c * jax.nn.sigmoid(c)
    o_ref[0] = jnp.dot(s, w_ref[0], preferred_element_type=F32) + b_ref[0]


def _modulation(cond, w_mod, b_mod):
    depth, d, d3 = w_mod.shape
    rows = cond.shape[0]
    tn = 1024
    return pl.pallas_call(
        _mod_kernel,
        grid=(depth, d3 // tn),
        in_specs=[
            pl.BlockSpec((rows, d), lambda l, j: (0, 0)),
            pl.BlockSpec((1, d, tn), lambda l, j: (l, 0, j)),
            pl.BlockSpec((1, 1, tn), lambda l, j: (l, 0, j)),
        ],
        out_specs=pl.BlockSpec((1, rows, tn), lambda l, j: (l, 0, j)),
        out_shape=jax.ShapeDtypeStruct((depth, rows, d3), F32),
        compiler_params=_cparams(("parallel", "parallel")),
        name="modulation",
    )(cond, w_mod, b_mod.reshape(depth, 1, d3))


def _rope_store(t, cos, sin, is_first, o_ref):
    for cidx in range(t.shape[1] // LANES):
        tc = t[:, cidx * LANES:(cidx + 1) * LANES]
        partner = jnp.where(is_first, pltpu.roll(tc, LANES - 16, axis=1), pltpu.roll(tc, 16, axis=1))
        o_ref[0, :, cidx * LANES:(cidx + 1) * LANES] = (tc * cos + partner * sin).astype(o_ref.dtype)


def _outproj_value(y_ref, z_ref, x_ref, g_ref, w_ref, lng_ref, lnb_ref, alpha):
    z = z_ref[0].astype(F32)
    u = (y_ref[0].astype(F32) * (z * jax.nn.sigmoid(z))).astype(BF16)
    out = jnp.dot(u, w_ref[...], preferred_element_type=F32)
    r = alpha * x_ref[0] + (1.0 + g_ref[0]) * out
    mu = jnp.mean(r, axis=1, keepdims=True)
    rc = r - mu
    var = jnp.mean(rc * rc, axis=1, keepdims=True)
    return rc * lax.rsqrt(var + LN_EPS) * lng_ref[...] + lnb_ref[...]


def _inproj_kernel(*refs, rope, with_qz, q_scale, alpha):
    it = iter(refs)
    prev = [next(it) for _ in range(7)] if alpha is not None else None
    x_ref = prev[2] if prev else next(it)
    sc_ref, sh_ref = next(it), next(it)
    wq_ref = next(it) if with_qz else None
    wk_ref, wv_ref = next(it), next(it)
    wz_ref = next(it) if with_qz else None
    cos_ref, sin_ref = (next(it), next(it)) if rope else (None, None)
    xo_ref = next(it) if prev else None
    q_ref = next(it) if with_qz else None
    k_ref, v_ref = next(it), next(it)
    z_ref = next(it) if with_qz else None

    if prev:
        x_val = _outproj_value(*prev, alpha)
        xo_ref[0] = x_val
    else:
        x_val = x_ref[0]
    h = (x_val * (1.0 + sc_ref[0]) + sh_ref[0]).astype(BF16)
    if rope:
        cos, sin = cos_ref[...], sin_ref[...]
        lane = lax.broadcasted_iota(jnp.int32, cos.shape, 1)
        is_first = (lane // 16) % 2 == 0
    if with_qz:
        q = jnp.dot(h, wq_ref[...], preferred_element_type=F32) * q_scale
        if rope:
            _rope_store(q, cos, sin, is_first, q_ref)
        else:
            q_ref[0] = q.astype(q_ref.dtype)
    k = jnp.dot(h, wk_ref[...], preferred_element_type=F32)
    if rope:
        _rope_store(k, cos, sin, is_first, k_ref)
    else:
        k_ref[0] = k.astype(k_ref.dtype)
    v_ref[0] = lax.dot_general(wv_ref[...], h, (((1,), (1,)), ((), ())),
                               preferred_element_type=F32).astype(v_ref.dtype)
    if with_qz:
        z_ref[0] = jnp.dot(h, wz_ref[...], preferred_element_type=F32).astype(z_ref.dtype)


def _inproj(x, sc, sh, w_all, layer, wv_t, *, tm, rope_tables, with_qz, q_scale, prev=None):
    b, n, d = x.shape
    w = w_all.shape[2] // 4
    per_batch = sc.shape[0] == b
    mod_map = (lambda i, t: (i, 0, 0)) if per_batch else (lambda i, t: (0, 0, 0))
    const = lambda i, t: (0, 0)
    rope = rope_tables is not None
    once = dict(pipeline_mode=pl.Buffered(1))
    wspec = lambda col: pl.BlockSpec((None, d, w), lambda i, t: (layer, 0, col), **once)
    row = lambda i, t: (i, t, 0)
    row_spec = pl.BlockSpec((1, tm, w), row)
    row_shape = jax.ShapeDtypeStruct((b, n, w), BF16)

    out_shape, out_specs = [], []
    if prev is None:
        args, in_specs, alpha = [x], [pl.BlockSpec((1, tm, d), row)], None
    else:
        y_prev, z_prev, gate, w_out_all, ln_g_all, ln_b_all, prev_layer, alpha = prev
        lay = lambda i, t: (prev_layer, 0, 0)
        args = [y_prev, z_prev, x, gate, w_out_all, ln_g_all, ln_b_all]
        in_specs = [row_spec, row_spec, pl.BlockSpec((1, tm, d), row), pl.BlockSpec((1, 1, d), mod_map),
                    pl.BlockSpec((None, w, d), lay, **once), pl.BlockSpec((None, 1, d), lay),
                    pl.BlockSpec((None, 1, d), lay)]
        out_shape.append(jax.ShapeDtypeStruct((b, n, d), F32))
        out_specs.append(pl.BlockSpec((1, tm, d), row))
    args += [sc, sh]
    in_specs += [pl.BlockSpec((1, 1, d), mod_map), pl.BlockSpec((1, 1, d), mod_map)]
    if with_qz:
        args.append(w_all); in_specs.append(wspec(0))
        out_shape.append(row_shape); out_specs.append(row_spec)
    args += [w_all, wv_t]
    in_specs += [wspec(1), pl.BlockSpec(wv_t.shape, const, **once)]
    out_shape.append(row_shape); out_specs.append(row_spec)
    out_shape.append(jax.ShapeDtypeStruct((b, w, n), BF16))
    out_specs.append(pl.BlockSpec((1, w, tm), lambda i, t: (i, 0, t)))
    if with_qz:
        args.append(w_all); in_specs.append(wspec(3))
        out_shape.append(row_shape); out_specs.append(row_spec)
    if rope:
        args += list(rope_tables)
        in_specs += [pl.BlockSpec((tm, LANES), lambda i, t: (t, 0))] * 2

    outs = pl.pallas_call(
        functools.partial(_inproj_kernel, rope=rope, with_qz=with_qz, q_scale=q_scale, alpha=alpha),
        grid=(b, n // tm),
        in_specs=in_specs,
        out_specs=out_specs,
        out_shape=out_shape,
        compiler_params=_cparams(("parallel", "parallel")),
        name="inproj",
    )(*args)
    outs = list(outs)
    x_new = [outs.pop(0)] if prev is not None else []
    if with_qz:
        return (*x_new, *outs)
    return (*x_new, None, outs[0], outs[1], None)


def _stack_heads(q, first_head):
    zero = jnp.zeros_like(q)
    return jnp.concatenate([jnp.where(first_head, q, zero), jnp.where(first_head, zero, q)], axis=0)


_NT = (((1,), (1,)), ((), ()))


def _na_kernel(q_ref, k_ref, vt_ref, kc_ref, vct_ref, bias_ref, o_ref, s_ref, *, rows):
    kc = kc_ref[0]
    n_ctx = kc.shape[0]
    first_head = lax.broadcasted_iota(jnp.int32, (GRID_W, LANES), 1) < HEAD_DIM
    band = NA_PAIR_BAND * GRID_W
    neg_tile = bias_ref.shape[1] - 1
    n_pairs = rows // 2
    assert rows % 2 == 0 and n_pairs % 4 == 0 and n_pairs >= 8

    def band_row(pair):
        return jnp.clip(2 * pair - NA_BAND_ROWS // 2, 0, rows - NA_PAIR_BAND)

    def stage_a(pair, slot):
        r = 2 * pair
        b0 = band_row(pair)
        q2 = q_ref[0, pl.ds(pl.multiple_of(r * GRID_W, 2 * GRID_W), 2 * GRID_W), :]
        qcat = jnp.concatenate([_stack_heads(q2[:GRID_W], first_head),
                                _stack_heads(q2[GRID_W:], first_head)], axis=0)
        kband = k_ref[0, pl.ds(pl.multiple_of(b0 * GRID_W, 2 * GRID_W), band), :]
        tiles = []
        for i in range(NA_PAIR_BAND):
            per_row = []
            for rho in range(2):
                start = jnp.clip(r + rho - NA_BAND_ROWS // 2, 0, rows - NA_BAND_ROWS)
                off = b0 + i - start
                dr = b0 + i - (r + rho) + NA_BAND_ROWS - 1
                idx = jnp.where((off >= 0) & (off < NA_BAND_ROWS), dr, neg_tile)
                per_row.append(bias_ref[0, idx])
            tiles.append(jnp.concatenate(per_row, axis=1))
        st = lax.dot_general(kband, qcat, _NT, preferred_element_type=F32) + jnp.concatenate(tiles, axis=0)
        sc = lax.dot_general(kc, qcat, _NT, preferred_element_type=F32)
        s_ref[slot, 0:band, :] = st
        s_ref[slot, band:, :] = sc
        return jnp.maximum(jnp.max(st, axis=0, keepdims=True), jnp.max(sc, axis=0, keepdims=True))

    def stage_b(pair, slot, m):
        r = 2 * pair
        b0 = band_row(pair)
        p = jnp.exp2(s_ref[slot] - m).astype(BF16)
        vband = vt_ref[0, :, pl.ds(pl.multiple_of(b0 * GRID_W, 2 * GRID_W), band)]
        vaug = jnp.concatenate([vband, vct_ref[0]], axis=1)
        vaug = jnp.concatenate([vaug, jnp.ones((SUBLANES_BF16, band + n_ctx), BF16)], axis=0)
        acc = jnp.dot(vaug, p, preferred_element_type=F32)
        ot = (acc[0:LANES] / acc[LANES:LANES + 1]).T
        y = jnp.concatenate(
            [jnp.where(first_head, ot[rho * LANES:rho * LANES + GRID_W], ot[rho * LANES + GRID_W:(rho + 1) * LANES])
             for rho in range(2)], axis=0)
        o_ref[0, pl.ds(pl.multiple_of(r * GRID_W, 2 * GRID_W), 2 * GRID_W), :] = y.astype(o_ref.dtype)

    m0 = stage_a(0, 0)
    m1 = stage_a(1, 1)

    def body(i, carry):
        m0, m1 = carry
        j = 4 * i
        stage_b(j, 0, m0)
        m2 = stage_a(j + 2, 2)
        stage_b(j + 1, 1, m1)
        m3 = stage_a(j + 3, 3)
        stage_b(j + 2, 2, m2)
        m0 = stage_a(j + 4, 0)
        stage_b(j + 3, 3, m3)
        m1 = stage_a(j + 5, 1)
        return m0, m1

    m0, m1 = lax.fori_loop(0, n_pairs // 4 - 1, body, (m0, m1))
    j = n_pairs - 4
    stage_b(j, 0, m0)
    m2 = stage_a(j + 2, 2)
    stage_b(j + 1, 1, m1)
    m3 = stage_a(j + 3, 3)
    stage_b(j + 2, 2, m2)
    stage_b(j + 3, 3, m3)


def _na_attention(q, k, vt, kc, vct, bias):
    b, n, w = q.shape
    c = kc.shape[1]
    rows = n // GRID_W
    hp = w // LANES
    tok = lambda i, j: (i, 0, j)
    chan = lambda i, j: (i, j, 0)
    return pl.pallas_call(
        functools.partial(_na_kernel, rows=rows),
        grid=(b, hp),
        in_specs=[
            pl.BlockSpec((1, n, LANES), tok), pl.BlockSpec((1, n, LANES), tok),
            pl.BlockSpec((1, LANES, n), chan),
            pl.BlockSpec((1, c, LANES), tok), pl.BlockSpec((1, LANES, c), chan),
            pl.BlockSpec((1,) + bias.shape[1:], lambda i, j: (j, 0, 0, 0)),
        ],
        out_specs=pl.BlockSpec((1, n, LANES), tok),
        out_shape=jax.ShapeDtypeStruct((b, n, w), BF16),
        scratch_shapes=[pltpu.VMEM((4, NA_PAIR_BAND * GRID_W + c, 2 * LANES), F32)],
        compiler_params=_cparams(("parallel", "parallel")),
        name="na_attention",
    )(q, k, vt, kc, vct, bias)


def _na_ctx_kernel(q_ref, kc_ref, vct_ref, o_ref):
    n = q_ref.shape[1]
    first_head = lax.broadcasted_iota(jnp.int32, (n, LANES), 1) < HEAD_DIM
    for hp in range(q_ref.shape[2] // LANES):
        lanes = slice(hp * LANES, (hp + 1) * LANES)
        s = lax.dot_general(_stack_heads(q_ref[0, :, lanes], first_head), kc_ref[0, :, lanes], _NT,
                            preferred_element_type=F32)
        m = jnp.max(s, axis=1, keepdims=True)
        p = jnp.exp2(s - m)
        o2 = (lax.dot_general(p.astype(BF16), vct_ref[0, lanes, :], _NT, preferred_element_type=F32)
              / jnp.sum(p, axis=1, keepdims=True))
        o_ref[0, :, lanes] = jnp.where(first_head, o2[:n], o2[n:]).astype(o_ref.dtype)


def _na_ctx_attention(qc, kc, vct):
    b, c, w = qc.shape
    tok = pl.BlockSpec((1, c, w), lambda i: (i, 0, 0))
    return pl.pallas_call(
        _na_ctx_kernel,
        grid=(b,),
        in_specs=[tok, tok, pl.BlockSpec((1, w, c), lambda i: (i, 0, 0))],
        out_specs=tok,
        out_shape=jax.ShapeDtypeStruct((b, c, w), BF16),
        compiler_params=_cparams(("parallel",)),
        name="na_ctx_attention",
    )(qc, kc, vct)


def _na_bias_table(rpb):
    h, n_dr, n_rel = rpb.shape
    kj = jnp.arange(GRID_W)[:, None]
    lane = jnp.arange(2 * GRID_W)[None, :]
    hd, wq = lane // GRID_W, lane % GRID_W
    col_start = jnp.clip(wq - NA_WIN_COLS // 2, 0, GRID_W - NA_WIN_COLS)
    valid = (kj >= col_start) & (kj < col_start + NA_WIN_COLS)
    rel = jnp.clip(kj - wq + NA_WIN_COLS - 1, 0, n_rel - 1)
    pairs = rpb.reshape(h // 2, 2, n_dr, n_rel).transpose(0, 2, 1, 3).reshape(h // 2, n_dr, 2 * n_rel)
    t = jnp.where(valid[None, None], pairs[:, :, hd * n_rel + rel], NEG)
    neg = jnp.full((h // 2, 1, GRID_W, 2 * GRID_W), NEG, F32)
    return jnp.concatenate([t.astype(F32), neg], axis=1)


def _diff_kernel(*refs, n_lat, kb, tq, lambda_init):
    it = iter(refs)
    q_ref = next(it)
    k_ref, vt_ref = (next(it), next(it)) if n_lat else (None, None)
    kc_ref, vct_ref, lam_ref, g_ref, o_ref, acc_ref, s_ref = (next(it) for _ in range(7))
    n_ctx = kc_ref.shape[1]
    n_tiles = q_ref.shape[1] // tq
    first_map = lax.broadcasted_iota(jnp.int32, (tq, LANES), 1) < HEAD_DIM
    lp = lam_ref[...]
    lam = (jnp.exp(jnp.sum(lp[0:1] * lp[1:2], axis=1, keepdims=True))
           - jnp.exp(jnp.sum(lp[2:3] * lp[3:4], axis=1, keepdims=True)) + lambda_init)

    def kblock(j):
        return k_ref[0, pl.ds(pl.multiple_of(j * kb, kb), kb), :]

    def vblock(j):
        return vt_ref[0, :, pl.ds(pl.multiple_of(j * kb, kb), kb)]

    for t in range(n_tiles):
        rows = slice(t * tq, (t + 1) * tq)
        qcat = _stack_heads(q_ref[0, rows, :], first_map)
        acc_ref[t] = jnp.zeros(acc_ref.shape[1:], F32)

        def scores(kblk, slot, t=t, qcat=qcat):
            st = lax.dot_general(kblk, qcat, _NT, preferred_element_type=F32)
            s_ref[DIFF_SLOTS * t + slot, 0:kblk.shape[0], :] = st
            return jnp.max(st, axis=0, keepdims=True)

        def consume(slot, vtblk, bmax, m_old, t=t):
            m_new = jnp.maximum(m_old, bmax)
            alpha = jnp.exp2(m_old - m_new)
            p = jnp.exp2(s_ref[DIFF_SLOTS * t + slot, 0:vtblk.shape[1], :] - m_new)
            vaug = jnp.concatenate([vtblk, jnp.ones((SUBLANES_BF16, vtblk.shape[1]), BF16)], axis=0)
            acc_ref[t] = alpha * acc_ref[t] + jnp.dot(vaug, p.astype(BF16), preferred_element_type=F32)
            return m_new

        m = jnp.full((1, 2 * tq), NEG, F32)
        if n_lat:
            nb = n_lat // kb
            assert nb >= 2 and (nb - 2) % DIFF_SLOTS == 0 and n_ctx <= kb
            bm = scores(kblock(0), 0)

            def body(i, carry, scores=scores, consume=consume):
                m, bm = carry
                for u in range(DIFF_SLOTS):
                    j = DIFF_SLOTS * i + u
                    bm_next = scores(kblock(j + 1), (u + 1) % DIFF_SLOTS)
                    m = consume(u, vblock(j), bm, m)
                    bm = bm_next
                return m, bm

            m, bm = lax.fori_loop(0, (nb - 2) // DIFF_SLOTS, body, (m, bm))
            bm_next = scores(kblock(nb - 1), (nb - 1) % DIFF_SLOTS)
            m = consume((nb - 2) % DIFF_SLOTS, vblock(nb - 2), bm, m)
            bm_ctx = scores(kc_ref[0], nb % DIFF_SLOTS)
            m = consume((nb - 1) % DIFF_SLOTS, vblock(nb - 1), bm_next, m)
            consume(nb % DIFF_SLOTS, vct_ref[0], bm_ctx, m)
        else:
            consume(0, vct_ref[0], scores(kc_ref[0], 0), m)

        o_all = acc_ref[t, 0:LANES, :] / acc_ref[t, LANES:LANES + 1, :]
        ot = o_all[:, :tq] - lam * o_all[:, tq:]
        ms = jnp.mean(ot * ot, axis=0, keepdims=True)
        ot = ot * lax.rsqrt(ms + LN_EPS)
        o_ref[0, rows, :] = (ot.T * g_ref[...] * (1.0 - lambda_init)).astype(o_ref.dtype)


def _diff_attention(q, k, vt, kc, vct, lam_params, subln_g, *, lambda_init, tq, kb, tiles=1):
    b, nq, w = q.shape
    c = kc.shape[1]
    heads = w // LANES
    n_lat = 0 if k is None else k.shape[1]
    step_rows = tiles * tq
    args = [q]
    in_specs = [pl.BlockSpec((1, step_rows, LANES), lambda i, h, t: (i, t, h))]
    if n_lat:
        args += [k, vt]
        in_specs += [pl.BlockSpec((1, n_lat, LANES), lambda i, h, t: (i, 0, h)),
                     pl.BlockSpec((1, LANES, n_lat), lambda i, h, t: (i, h, 0))]
    args += [kc, vct, lam_params, subln_g.reshape(1, LANES)]
    in_specs += [pl.BlockSpec((1, c, LANES), lambda i, h, t: (i, 0, h)),
                 pl.BlockSpec((1, LANES, c), lambda i, h, t: (i, h, 0)),
                 pl.BlockSpec(lam_params.shape, lambda i, h, t: (0, 0)),
                 pl.BlockSpec((1, LANES), lambda i, h, t: (0, 0))]
    return pl.pallas_call(
        functools.partial(_diff_kernel, n_lat=n_lat, kb=kb, tq=tq, lambda_init=lambda_init),
        grid=(b, heads, nq // step_rows),
        in_specs=in_specs,
        out_specs=pl.BlockSpec((1, step_rows, LANES), lambda i, h, t: (i, t, h)),
        out_shape=jax.ShapeDtypeStruct((b, nq, w), BF16),
        scratch_shapes=[pltpu.VMEM((tiles, LANES + SUBLANES_BF16, 2 * tq), F32),
                        pltpu.VMEM((DIFF_SLOTS * tiles, kb, 2 * tq), F32)],
        compiler_params=_cparams(("parallel", "parallel", "arbitrary")),
        name="diff_attention",
    )(*args)


def _outproj_kernel(y_ref, z_ref, x_ref, g_ref, w_ref, lng_ref, lnb_ref, o_ref, *, alpha):
    o_ref[0] = _outproj_value(y_ref, z_ref, x_ref, g_ref, w_ref, lng_ref, lnb_ref, alpha)


def _outproj(y, z, x, gate, w_out_all, ln_g_all, ln_b_all, layer, *, tm, alpha):
    b, n, d = x.shape
    w = y.shape[2]
    per_batch = gate.shape[0] == b
    mod_map = (lambda i, t: (i, 0, 0)) if per_batch else (lambda i, t: (0, 0, 0))
    row = lambda i, t: (i, t, 0)
    lay = lambda i, t: (layer, 0, 0)
    return pl.pallas_call(
        functools.partial(_outproj_kernel, alpha=alpha),
        grid=(b, n // tm),
        in_specs=[
            pl.BlockSpec((1, tm, w), row), pl.BlockSpec((1, tm, w), row),
            pl.BlockSpec((1, tm, d), row), pl.BlockSpec((1, 1, d), mod_map),
            pl.BlockSpec((None, w, d), lay), pl.BlockSpec((None, 1, d), lay), pl.BlockSpec((None, 1, d), lay),
        ],
        out_specs=pl.BlockSpec((1, tm, d), row),
        out_shape=jax.ShapeDtypeStruct((b, n, d), F32),
        compiler_params=_cparams(("parallel", "parallel")),
        name="outproj_ln",
    )(y, z, x, gate, w_out_all, ln_g_all, ln_b_all)


def _rope_tables(n_tok):
    t = jnp.arange(n_tok, dtype=jnp.int32)
    row = (t // GRID_W).astype(F32)[:, None]
    col = (t % GRID_W).astype(F32)[:, None]
    n_freq = HEAD_DIM // 4
    inv_freq = ROPE_BASE ** (-jnp.arange(n_freq, dtype=F32) / n_freq)
    lane = jnp.arange(LANES)
    pos = jnp.where(((lane % HEAD_DIM) // (HEAD_DIM // 2) == 0)[None, :], row, col)
    ang = pos * inv_freq[lane % n_freq][None, :]
    first = ((lane // n_freq) % 2 == 0)[None, :]
    return jnp.cos(ang), jnp.where(first, -jnp.sin(ang), jnp.sin(ang))


def kernel(x, c, ctx, c_ctx, w_mod, b_mod, w_in, w_out, ln_g, ln_b, na_rpb, diff_lambda, diff_subln_g):
    depth, d, _ = w_mod.shape
    b, n, _ = x.shape
    w = w_out.shape[1]
    n_ctx = ctx.shape[1]
    alpha = (2.0 * depth) ** 0.25
    q_scale = HEAD_DIM ** -0.5 * LOG2E

    cond = jnp.concatenate([c, c_ctx[None, :], jnp.zeros((16 - b - 1, d), F32)], axis=0)
    mods = _modulation(cond, w_mod, b_mod)
    w_in_bf = w_in.astype(BF16)
    w_out_bf = w_out.astype(BF16)
    ln_g3, ln_b3 = ln_g[:, None, :], ln_b[:, None, :]
    rope_tables = _rope_tables(n)
    wv_t_all = jnp.swapaxes(w_in[:, :, 2 * w:3 * w], 1, 2).astype(BF16)

    prev = prev_c = None
    for l in range(depth):
        need_ctx = l < depth - 1
        is_diff = l % 2 == 1
        m = mods[l]
        sh, sc, g = (m[:b, None, i * d:(i + 1) * d] for i in range(3))
        shc, scc, gc = (m[b:b + 1, None, i * d:(i + 1) * d] for i in range(3))
        wv_t = wv_t_all[l]
        res = _inproj(x, sc, sh, w_in_bf, l, wv_t, tm=512, rope_tables=rope_tables if is_diff else None,
                      with_qz=True, q_scale=q_scale, prev=prev)
        if prev is not None:
            x, res = res[0], res[1:]
        q, k, v, z = res
        res = _inproj(ctx, scc, shc, w_in_bf, l, wv_t, tm=n_ctx, rope_tables=None,
                      with_qz=need_ctx, q_scale=q_scale, prev=prev_c)
        if prev_c is not None:
            ctx, res = res[0], res[1:]
        qc, kc, vc, zc = res
        if is_diff:
            lam_p, sub_g = diff_lambda[l // 2], diff_subln_g[l // 2]
            lambda_init = 0.8 - 0.6 * math.exp(-0.3 * l)
            y = _diff_attention(q, k, v, kc, vc, lam_p, sub_g, lambda_init=lambda_init, tq=512, kb=512, tiles=4)
            if need_ctx:
                yc = _diff_attention(qc, None, None, kc, vc, lam_p, sub_g,
                                     lambda_init=lambda_init, tq=n_ctx, kb=512)
        else:
            y = _na_attention(q, k, v, kc, vc, _na_bias_table(na_rpb[l // 2] * LOG2E))
            if need_ctx:
                yc = _na_ctx_attention(qc, kc, vc)
        prev = (y, z, g, w_out_bf, ln_g3, ln_b3, l, alpha)
        prev_c = (yc, zc, gc, w_out_bf, ln_g3, ln_b3, l, alpha) if need_ctx else None
    y, z, g, w_out_all, ln_g_all, ln_b_all, last = prev[:7]
    return _outproj(y, z, x, g, w_out_all, ln_g_all, ln_b_all, last, tm=512, alpha=alpha)
```

```python
import functools
import math

import jax
import jax.numpy as jnp
from jax import lax
from jax.experimental import pallas as pl
from jax.experimental.pallas import tpu as pltpu

F32 = jnp.float32
BF16 = jnp.bfloat16

GRID_W = 64
ROPE_BASE = 10000.0
LN_EPS = 1e-5
NEG = -1e30
LANES = 128
HEAD_DIM = 64
NA_BAND_ROWS = 8
NA_WIN_COLS = 16
NA_TRIP_PAIRS = 8
NA_PAIR_BAND = 10
SUBLANES_BF16 = 16
DIFF_SLOTS = 3
LOG2E = math.log2(math.e)
VMEM_LIMIT = 48 * 1024 * 1024


def _cparams(sem):
    return pltpu.CompilerParams(dimension_semantics=sem, vmem_limit_bytes=VMEM_LIMIT)


def _mod_kernel(c_ref, w_ref, b_ref, o_ref):
    c = c_ref[...]
    s = c * jax.nn.sigmoid(c)
    o_ref[0] = jnp.dot(s, w_ref[0], preferred_element_type=F32) + b_ref[0]


def _modulation(cond, w_mod, b_mod):
    depth, d, d3 = w_mod.shape
    rows = cond.shape[0]
    tn = 1024
    return pl.pallas_call(
        _mod_kernel,
        grid=(depth, d3 // tn),
        in_specs=[
            pl.BlockSpec((rows, d), lambda l, j: (0, 0)),
            pl.BlockSpec((1, d, tn), lambda l, j: (l, 0, j)),
            pl.BlockSpec((1, 1, tn), lambda l, j: (l, 0, j)),
        ],
        out_specs=pl.BlockSpec((1, rows, tn), lambda l, j: (l, 0, j)),
        out_shape=jax.ShapeDtypeStruct((depth, rows, d3), F32),
        compiler_params=_cparams(("parallel", "parallel")),
        name="modulation",
    )(cond, w_mod, b_mod.reshape(depth, 1, d3))


def _cast_kernel(w_ref, o_ref):
    o_ref[...] = w_ref[...].astype(o_ref.dtype)


def _to_bf16(wts, tn):
    depth, rows, cols = wts.shape
    spec = pl.BlockSpec((1, rows, tn), lambda l, j: (l, 0, j))
    return pl.pallas_call(
        _cast_kernel,
        grid=(depth, cols // tn),
        in_specs=[spec],
        out_specs=spec,
        out_shape=jax.ShapeDtypeStruct(wts.shape, BF16),
        compiler_params=_cparams(("parallel", "parallel")),
        name="weights_to_bf16",
    )(wts)


def _rope_store(t, cos, sin, is_first, o_ref):
    for cidx in range(t.shape[1] // LANES):
        tc = t[:, cidx * LANES:(cidx + 1) * LANES]
        partner = jnp.where(is_first, pltpu.roll(tc, LANES - 16, axis=1), pltpu.roll(tc, 16, axis=1))
        o_ref[0, :, cidx * LANES:(cidx + 1) * LANES] = (tc * cos + partner * sin).astype(o_ref.dtype)


def _outproj_value(y_ref, z_ref, x_ref, g_ref, w_ref, lng_ref, lnb_ref, alpha):
    z = z_ref[0].astype(F32)
    u = (y_ref[0].astype(F32) * (z * jax.nn.sigmoid(z))).astype(BF16)
    out = jnp.dot(u, w_ref[...], preferred_element_type=F32)
    r = alpha * x_ref[0] + (1.0 + g_ref[0]) * out
    mu = jnp.mean(r, axis=1, keepdims=True)
    rc = r - mu
    var = jnp.mean(rc * rc, axis=1, keepdims=True)
    return rc * lax.rsqrt(var + LN_EPS) * lng_ref[...] + lnb_ref[...]


def _inproj_kernel(*refs, rope, with_qz, q_scale, alpha):
    it = iter(refs)
    prev = [next(it) for _ in range(7)] if alpha is not None else None
    x_ref = prev[2] if prev else next(it)
    sc_ref, sh_ref = next(it), next(it)
    wq_ref = next(it) if with_qz else None
    wk_ref, wv_ref = next(it), next(it)
    wz_ref = next(it) if with_qz else None
    cos_ref, sin_ref = (next(it), next(it)) if rope else (None, None)
    xo_ref = next(it) if prev else None
    q_ref = next(it) if with_qz else None
    k_ref, v_ref = next(it), next(it)
    z_ref = next(it) if with_qz else None

    if prev:
        x_val = _outproj_value(*prev, alpha)
        xo_ref[0] = x_val
    else:
        x_val = x_ref[0]
    h = (x_val * (1.0 + sc_ref[0]) + sh_ref[0]).astype(BF16)
    if rope:
        cos, sin = cos_ref[...], sin_ref[...]
        lane = lax.broadcasted_iota(jnp.int32, cos.shape, 1)
        is_first = (lane // 16) % 2 == 0
    if with_qz:
        q = jnp.dot(h, wq_ref[...], preferred_element_type=F32) * q_scale
        if rope:
            _rope_store(q, cos, sin, is_first, q_ref)
        else:
            q_ref[0] = q.astype(q_ref.dtype)
    k = jnp.dot(h, wk_ref[...], preferred_element_type=F32)
    if rope:
        _rope_store(k, cos, sin, is_first, k_ref)
    else:
        k_ref[0] = k.astype(k_ref.dtype)
    v_ref[0] = lax.dot_general(wv_ref[...], h, (((1,), (1,)), ((), ())),
                               preferred_element_type=F32).astype(v_ref.dtype)
    if with_qz:
        z_ref[0] = jnp.dot(h, wz_ref[...], preferred_element_type=F32).astype(z_ref.dtype)


def _inproj(x, sc, sh, w_all, layer, wv_t, *, tm, rope_tables, with_qz, q_scale, prev=None):
    b, n, d = x.shape
    w = w_all.shape[2] // 4
    per_batch = sc.shape[0] == b
    mod_map = (lambda i, t: (i, 0, 0)) if per_batch else (lambda i, t: (0, 0, 0))
    const = lambda i, t: (0, 0)
    rope = rope_tables is not None
    once = dict(pipeline_mode=pl.Buffered(1))
    wspec = lambda col: pl.BlockSpec((None, d, w), lambda i, t: (layer, 0, col), **once)
    row = lambda i, t: (i, t, 0)
    row_spec = pl.BlockSpec((1, tm, w), row)
    row_shape = jax.ShapeDtypeStruct((b, n, w), BF16)

    out_shape, out_specs = [], []
    if prev is None:
        args, in_specs, alpha = [x], [pl.BlockSpec((1, tm, d), row)], None
    else:
        y_prev, z_prev, gate, w_out_all, ln_g_all, ln_b_all, prev_layer, alpha = prev
        lay = lambda i, t: (prev_layer, 0, 0)
        args = [y_prev, z_prev, x, gate, w_out_all, ln_g_all, ln_b_all]
        in_specs = [row_spec, row_spec, pl.BlockSpec((1, tm, d), row), pl.BlockSpec((1, 1, d), mod_map),
                    pl.BlockSpec((None, w, d), lay, **once), pl.BlockSpec((None, 1, d), lay),
                    pl.BlockSpec((None, 1, d), lay)]
        out_shape.append(jax.ShapeDtypeStruct((b, n, d), F32))
        out_specs.append(pl.BlockSpec((1, tm, d), row))
    args += [sc, sh]
    in_specs += [pl.BlockSpec((1, 1, d), mod_map), pl.BlockSpec((1, 1, d), mod_map)]
    if with_qz:
        args.append(w_all); in_specs.append(wspec(0))
        out_shape.append(row_shape); out_specs.append(row_spec)
    args += [w_all, wv_t]
    in_specs += [wspec(1), pl.BlockSpec(wv_t.shape, const, **once)]
    out_shape.append(row_shape); out_specs.append(row_spec)
    out_shape.append(jax.ShapeDtypeStruct((b, w, n), BF16))
    out_specs.append(pl.BlockSpec((1, w, tm), lambda i, t: (i, 0, t)))
    if with_qz:
        args.append(w_all); in_specs.append(wspec(3))
        out_shape.append(row_shape); out_specs.append(row_spec)
    if rope:
        args += list(rope_tables)
        in_specs += [pl.BlockSpec((tm, LANES), lambda i, t: (t, 0))] * 2

    outs = pl.pallas_call(
        functools.partial(_inproj_kernel, rope=rope, with_qz=with_qz, q_scale=q_scale, alpha=alpha),
        grid=(b, n // tm),
        in_specs=in_specs,
        out_specs=out_specs,
        out_shape=out_shape,
        compiler_params=_cparams(("parallel", "parallel")),
        name="inproj",
    )(*args)
    outs = list(outs)
    x_new = [outs.pop(0)] if prev is not None else []
    if with_qz:
        return (*x_new, *outs)
    return (*x_new, None, outs[0], outs[1], None)


def _stack_heads(q, first_head):
    zero = jnp.zeros_like(q)
    return jnp.concatenate([jnp.where(first_head, q, zero), jnp.where(first_head, zero, q)], axis=0)


_NT = (((1,), (1,)), ((), ()))


def _na_kernel(q_ref, k_ref, vt_ref, kc_ref, vct_ref, bias_ref, o_ref, s_ref, *, rows):
    kc = kc_ref[0]
    n_ctx = kc.shape[0]
    first_head = lax.broadcasted_iota(jnp.int32, (GRID_W, LANES), 1) < HEAD_DIM
    band = NA_PAIR_BAND * GRID_W
    neg_tile = bias_ref.shape[1] - 1
    n_pairs = rows // 2
    assert rows % 2 == 0 and n_pairs >= 2 * NA_TRIP_PAIRS

    def band_row(pair):
        return jnp.clip(2 * pair - NA_BAND_ROWS // 2, 0, rows - NA_PAIR_BAND)

    def stage_a(pair, slot):
        r = 2 * pair
        b0 = band_row(pair)
        q2 = q_ref[0, pl.ds(pl.multiple_of(r * GRID_W, 2 * GRID_W), 2 * GRID_W), :]
        qcat = jnp.concatenate([_stack_heads(q2[:GRID_W], first_head),
                                _stack_heads(q2[GRID_W:], first_head)], axis=0)
        kband = k_ref[0, pl.ds(pl.multiple_of(b0 * GRID_W, 2 * GRID_W), band), :]
        tiles = []
        for i in range(NA_PAIR_BAND):
            per_row = []
            for rho in range(2):
                start = jnp.clip(r + rho - NA_BAND_ROWS // 2, 0, rows - NA_BAND_ROWS)
                off = b0 + i - start
                dr = b0 + i - (r + rho) + NA_BAND_ROWS - 1
                idx = jnp.where((off >= 0) & (off < NA_BAND_ROWS), dr, neg_tile)
                per_row.append(bias_ref[0, idx])
            tiles.append(jnp.concatenate(per_row, axis=1))
        st = lax.dot_general(kband, qcat, _NT, preferred_element_type=F32) + jnp.concatenate(tiles, axis=0)
        sc = lax.dot_general(kc, qcat, _NT, preferred_element_type=F32)
        s_ref[slot, 0:band, :] = st
        s_ref[slot, band:, :] = sc
        return jnp.maximum(jnp.max(st, axis=0, keepdims=True), jnp.max(sc, axis=0, keepdims=True))

    def stage_b(pair, slot, m):
        r = 2 * pair
        b0 = band_row(pair)
        p = jnp.exp2(s_ref[slot] - m).astype(BF16)
        vband = vt_ref[0, :, pl.ds(pl.multiple_of(b0 * GRID_W, 2 * GRID_W), band)]
        vaug = jnp.concatenate([vband, vct_ref[0]], axis=1)
        vaug = jnp.concatenate([vaug, jnp.ones((SUBLANES_BF16, band + n_ctx), BF16)], axis=0)
        acc = jnp.dot(vaug, p, preferred_element_type=F32)
        ot = (acc[0:LANES] / acc[LANES:LANES + 1]).T
        y = jnp.concatenate(
            [jnp.where(first_head, ot[rho * LANES:rho * LANES + GRID_W], ot[rho * LANES + GRID_W:(rho + 1) * LANES])
             for rho in range(2)], axis=0)
        o_ref[0, pl.ds(pl.multiple_of(r * GRID_W, 2 * GRID_W), 2 * GRID_W), :] = y.astype(o_ref.dtype)

    trip = NA_TRIP_PAIRS
    assert n_pairs % trip == 0 and trip % 4 == 0
    ms = [stage_a(0, 0), stage_a(1, 1)]

    def steps(j0, count, ms, last_trip):
        for u in range(count):
            stage_b(j0 + u, u % 4, ms[0])
            nxt = [stage_a(j0 + u + 2, (u + 2) % 4)] if (not last_trip or u + 2 < count) else []
            ms = ms[1:] + nxt
        return ms

    def body(i, carry):
        return tuple(steps(trip * i, trip, list(carry), False))

    ms = list(lax.fori_loop(0, n_pairs // trip - 1, body, tuple(ms)))
    steps(n_pairs - trip, trip, ms, True)


def _na_attention(q, k, vt, kc, vct, bias):
    b, n, w = q.shape
    c = kc.shape[1]
    rows = n // GRID_W
    hp = w // LANES
    tok = lambda i, j: (i, 0, j)
    chan = lambda i, j: (i, j, 0)
    return pl.pallas_call(
        functools.partial(_na_kernel, rows=rows),
        grid=(b, hp),
        in_specs=[
            pl.BlockSpec((1, n, LANES), tok), pl.BlockSpec((1, n, LANES), tok),
            pl.BlockSpec((1, LANES, n), chan),
            pl.BlockSpec((1, c, LANES), tok), pl.BlockSpec((1, LANES, c), chan),
            pl.BlockSpec((1,) + bias.shape[1:], lambda i, j: (j, 0, 0, 0)),
        ],
        out_specs=pl.BlockSpec((1, n, LANES), tok),
        out_shape=jax.ShapeDtypeStruct((b, n, w), BF16),
        scratch_shapes=[pltpu.VMEM((4, NA_PAIR_BAND * GRID_W + c, 2 * LANES), F32)],
        compiler_params=_cparams(("parallel", "parallel")),
        name="na_attention",
    )(q, k, vt, kc, vct, bias)


def _na_ctx_kernel(q_ref, kc_ref, vct_ref, o_ref):
    n = q_ref.shape[1]
    first_head = lax.broadcasted_iota(jnp.int32, (n, LANES), 1) < HEAD_DIM
    for hp in range(q_ref.shape[2] // LANES):
        lanes = slice(hp * LANES, (hp + 1) * LANES)
        s = lax.dot_general(_stack_heads(q_ref[0, :, lanes], first_head), kc_ref[0, :, lanes], _NT,
                            preferred_element_type=F32)
        m = jnp.max(s, axis=1, keepdims=True)
        p = jnp.exp2(s - m)
        o2 = (lax.dot_general(p.astype(BF16), vct_ref[0, lanes, :], _NT, preferred_element_type=F32)
              / jnp.sum(p, axis=1, keepdims=True))
        o_ref[0, :, lanes] = jnp.where(first_head, o2[:n], o2[n:]).astype(o_ref.dtype)


def _na_ctx_attention(qc, kc, vct):
    b, c, w = qc.shape
    tok = pl.BlockSpec((1, c, w), lambda i: (i, 0, 0))
    return pl.pallas_call(
        _na_ctx_kernel,
        grid=(b,),
        in_specs=[tok, tok, pl.BlockSpec((1, w, c), lambda i: (i, 0, 0))],
        out_specs=tok,
        out_shape=jax.ShapeDtypeStruct((b, c, w), BF16),
        compiler_params=_cparams(("parallel",)),
        name="na_ctx_attention",
    )(qc, kc, vct)


def _na_bias_table(rpb):
    h, n_dr, n_rel = rpb.shape
    kj = jnp.arange(GRID_W)[:, None]
    lane = jnp.arange(2 * GRID_W)[None, :]
    hd, wq = lane // GRID_W, lane % GRID_W
    col_start = jnp.clip(wq - NA_WIN_COLS // 2, 0, GRID_W - NA_WIN_COLS)
    valid = (kj >= col_start) & (kj < col_start + NA_WIN_COLS)
    rel = jnp.clip(kj - wq + NA_WIN_COLS - 1, 0, n_rel - 1)
    pairs = rpb.reshape(h // 2, 2, n_dr, n_rel).transpose(0, 2, 1, 3).reshape(h // 2, n_dr, 2 * n_rel)
    t = jnp.where(valid[None, None], pairs[:, :, hd * n_rel + rel], NEG)
    neg = jnp.full((h // 2, 1, GRID_W, 2 * GRID_W), NEG, F32)
    return jnp.concatenate([t.astype(F32), neg], axis=1)


def _diff_kernel(*refs, n_lat, kb, tq, lambda_init):
    it = iter(refs)
    q_ref = next(it)
    k_ref, vt_ref = (next(it), next(it)) if n_lat else (None, None)
    kc_ref, vct_ref, lam_ref, g_ref, o_ref, acc_ref, s_ref = (next(it) for _ in range(7))
    n_ctx = kc_ref.shape[1]
    n_tiles = q_ref.shape[1] // tq
    first_map = lax.broadcasted_iota(jnp.int32, (tq, LANES), 1) < HEAD_DIM
    lp = lam_ref[...]
    lam = (jnp.exp(jnp.sum(lp[0:1] * lp[1:2], axis=1, keepdims=True))
           - jnp.exp(jnp.sum(lp[2:3] * lp[3:4], axis=1, keepdims=True)) + lambda_init)

    def kblock(j):
        return k_ref[0, pl.ds(pl.multiple_of(j * kb, kb), kb), :]

    def vblock(j):
        return vt_ref[0, :, pl.ds(pl.multiple_of(j * kb, kb), kb)]

    for t in range(n_tiles):
        rows = slice(t * tq, (t + 1) * tq)
        qcat = _stack_heads(q_ref[0, rows, :], first_map)
        acc_ref[t] = jnp.zeros(acc_ref.shape[1:], F32)

        def scores(kblk, slot, t=t, qcat=qcat):
            st = lax.dot_general(kblk, qcat, _NT, preferred_element_type=F32)
            s_ref[DIFF_SLOTS * t + slot, 0:kblk.shape[0], :] = st
            return jnp.max(st, axis=0, keepdims=True)

        def consume(slot, vtblk, bmax, m_old, t=t):
            m_new = jnp.maximum(m_old, bmax)
            alpha = jnp.exp2(m_old - m_new)
            p = jnp.exp2(s_ref[DIFF_SLOTS * t + slot, 0:vtblk.shape[1], :] - m_new)
            vaug = jnp.concatenate([vtblk, jnp.ones((SUBLANES_BF16, vtblk.shape[1]), BF16)], axis=0)
            acc_ref[t] = alpha * acc_ref[t] + jnp.dot(vaug, p.astype(BF16), preferred_element_type=F32)
            return m_new

        m = jnp.full((1, 2 * tq), NEG, F32)
        if n_lat:
            nb = n_lat // kb
            assert nb >= 2 and (nb - 2) % DIFF_SLOTS == 0 and n_ctx <= kb
            bm = scores(kblock(0), 0)

            def body(i, carry, scores=scores, consume=consume):
                m, bm = carry
                for u in range(DIFF_SLOTS):
                    j = DIFF_SLOTS * i + u
                    bm_next = scores(kblock(j + 1), (u + 1) % DIFF_SLOTS)
                    m = consume(u, vblock(j), bm, m)
                    bm = bm_next
                return m, bm

            m, bm = lax.fori_loop(0, (nb - 2) // DIFF_SLOTS, body, (m, bm))
            bm_next = scores(kblock(nb - 1), (nb - 1) % DIFF_SLOTS)
            m = consume((nb - 2) % DIFF_SLOTS, vblock(nb - 2), bm, m)
            bm_ctx = scores(kc_ref[0], nb % DIFF_SLOTS)
            m = consume((nb - 1) % DIFF_SLOTS, vblock(nb - 1), bm_next, m)
            consume(nb % DIFF_SLOTS, vct_ref[0], bm_ctx, m)
        else:
            consume(0, vct_ref[0], scores(kc_ref[0], 0), m)

        o_all = acc_ref[t, 0:LANES, :] / acc_ref[t, LANES:LANES + 1, :]
        ot = o_all[:, :tq] - lam * o_all[:, tq:]
        ms = jnp.mean(ot * ot, axis=0, keepdims=True)
        ot = ot * lax.rsqrt(ms + LN_EPS)
        o_ref[0, rows, :] = (ot.T * g_ref[...] * (1.0 - lambda_init)).astype(o_ref.dtype)


def _diff_attention(q, k, vt, kc, vct, lam_params, subln_g, *, lambda_init, tq, kb, tiles=1):
    b, nq, w = q.shape
    c = kc.shape[1]
    heads = w // LANES
    n_lat = 0 if k is None else k.shape[1]
    step_rows = tiles * tq
    args = [q]
    in_specs = [pl.BlockSpec((1, step_rows, LANES), lambda i, h, t: (i, t, h))]
    if n_lat:
        args += [k, vt]
        in_specs += [pl.BlockSpec((1, n_lat, LANES), lambda i, h, t: (i, 0, h)),
                     pl.BlockSpec((1, LANES, n_lat), lambda i, h, t: (i, h, 0))]
    args += [kc, vct, lam_params, subln_g.reshape(1, LANES)]
    in_specs += [pl.BlockSpec((1, c, LANES), lambda i, h, t: (i, 0, h)),
                 pl.BlockSpec((1, LANES, c), lambda i, h, t: (i, h, 0)),
                 pl.BlockSpec(lam_params.shape, lambda i, h, t: (0, 0)),
                 pl.BlockSpec((1, LANES), lambda i, h, t: (0, 0))]
    return pl.pallas_call(
        functools.partial(_diff_kernel, n_lat=n_lat, kb=kb, tq=tq, lambda_init=lambda_init),
        grid=(b, heads, nq // step_rows),
        in_specs=in_specs,
        out_specs=pl.BlockSpec((1, step_rows, LANES), lambda i, h, t: (i, t, h)),
        out_shape=jax.ShapeDtypeStruct((b, nq, w), BF16),
        scratch_shapes=[pltpu.VMEM((tiles, LANES + SUBLANES_BF16, 2 * tq), F32),
                        pltpu.VMEM((DIFF_SLOTS * tiles, kb, 2 * tq), F32)],
        compiler_params=_cparams(("parallel", "parallel", "arbitrary")),
        name="diff_attention",
    )(*args)


def _outproj_kernel(y_ref, z_ref, x_ref, g_ref, w_ref, lng_ref, lnb_ref, o_ref, *, alpha):
    o_ref[0] = _outproj_value(y_ref, z_ref, x_ref, g_ref, w_ref, lng_ref, lnb_ref, alpha)


def _outproj(y, z, x, gate, w_out_all, ln_g_all, ln_b_all, layer, *, tm, alpha):
    b, n, d = x.shape
    w = y.shape[2]
    per_batch = gate.shape[0] == b
    mod_map = (lambda i, t: (i, 0, 0)) if per_batch else (lambda i, t: (0, 0, 0))
    row = lambda i, t: (i, t, 0)
    lay = lambda i, t: (layer, 0, 0)
    return pl.pallas_call(
        functools.partial(_outproj_kernel, alpha=alpha),
        grid=(b, n // tm),
        in_specs=[
            pl.BlockSpec((1, tm, w), row), pl.BlockSpec((1, tm, w), row),
            pl.BlockSpec((1, tm, d), row), pl.BlockSpec((1, 1, d), mod_map),
            pl.BlockSpec((None, w, d), lay), pl.BlockSpec((None, 1, d), lay), pl.BlockSpec((None, 1, d), lay),
        ],
        out_specs=pl.BlockSpec((1, tm, d), row),
        out_shape=jax.ShapeDtypeStruct((b, n, d), F32),
        compiler_params=_cparams(("parallel", "parallel")),
        name="outproj_ln",
    )(y, z, x, gate, w_out_all, ln_g_all, ln_b_all)


def _rope_tables(n_tok):
    t = jnp.arange(n_tok, dtype=jnp.int32)
    row = (t // GRID_W).astype(F32)[:, None]
    col = (t % GRID_W).astype(F32)[:, None]
    n_freq = HEAD_DIM // 4
    inv_freq = ROPE_BASE ** (-jnp.arange(n_freq, dtype=F32) / n_freq)
    lane = jnp.arange(LANES)
    pos = jnp.where(((lane % HEAD_DIM) // (HEAD_DIM // 2) == 0)[None, :], row, col)
    ang = pos * inv_freq[lane % n_freq][None, :]
    first = ((lane // n_freq) % 2 == 0)[None, :]
    return jnp.cos(ang), jnp.where(first, -jnp.sin(ang), jnp.sin(ang))


def kernel(x, c, ctx, c_ctx, w_mod, b_mod, w_in, w_out, ln_g, ln_b, na_rpb, diff_lambda, diff_subln_g):
    depth, d, _ = w_mod.shape
    b, n, _ = x.shape
    w = w_out.shape[1]
    n_ctx = ctx.shape[1]
    alpha = (2.0 * depth) ** 0.25
    q_scale = HEAD_DIM ** -0.5 * LOG2E

    cond = jnp.concatenate([c, c_ctx[None, :], jnp.zeros((16 - b - 1, d), F32)], axis=0)
    mods = _modulation(cond, w_mod, b_mod)
    w_in_bf = _to_bf16(w_in, 1024)
    w_out_bf = _to_bf16(w_out, 1024)
    ln_g3, ln_b3 = ln_g[:, None, :], ln_b[:, None, :]
    rope_tables = _rope_tables(n)
    wv_t_all = jnp.swapaxes(w_in[:, :, 2 * w:3 * w], 1, 2).astype(BF16)

    prev = prev_c = None
    for l in range(depth):
        need_ctx = l < depth - 1
        is_diff = l % 2 == 1
        m = mods[l]
        sh, sc, g = (m[:b, None, i * d:(i + 1) * d] for i in range(3))
        shc, scc, gc = (m[b:b + 1, None, i * d:(i + 1) * d] for i in range(3))
        wv_t = wv_t_all[l]
        res = _inproj(x, sc, sh, w_in_bf, l, wv_t, tm=512, rope_tables=rope_tables if is_diff else None,
                      with_qz=True, q_scale=q_scale, prev=prev)
        if prev is not None:
            x, res = res[0], res[1:]
        q, k, v, z = res
        res = _inproj(ctx, scc, shc, w_in_bf, l, wv_t, tm=n_ctx, rope_tables=None,
                      with_qz=need_ctx, q_scale=q_scale, prev=prev_c)
        if prev_c is not None:
            ctx, res = res[0], res[1:]
        qc, kc, vc, zc = res
        if is_diff:
            lam_p, sub_g = diff_lambda[l // 2], diff_subln_g[l // 2]
            lambda_init = 0.8 - 0.6 * math.exp(-0.3 * l)
            y = _diff_attention(q, k, v, kc, vc, lam_p, sub_g, lambda_init=lambda_init, tq=512, kb=512, tiles=4)
            if need_ctx:
                yc = _diff_attention(qc, None, None, kc, vc, lam_p, sub_g,
                                     lambda_init=lambda_init, tq=n_ctx, kb=512)
        else:
            y = _na_attention(q, k, v, kc, vc, _na_bias_table(na_rpb[l // 2] * LOG2E))
            if need_ctx:
                yc = _na_ctx_attention(qc, kc, vc)
        prev = (y, z, g, w_out_bf, ln_g3, ln_b3, l, alpha)
        prev_c = (yc, zc, gc, w_out_bf, ln_g3, ln_b3, l, alpha) if need_ctx else None
    y, z, g, w_out_all, ln_g_all, ln_b_all, last = prev[:7]
    return _outproj(y, z, x, g, w_out_all, ln_g_all, ln_b_all, last, tm=512, alpha=alpha)
```

```python
import functools
import math

import jax
import jax.numpy as jnp
from jax import lax
from jax.experimental import pallas as pl
from jax.experimental.pallas import tpu as pltpu

F32 = jnp.float32
BF16 = jnp.bfloat16

GRID_W = 64
ROPE_BASE = 10000.0
LN_EPS = 1e-5
NEG = -1e30
LANES = 128
HEAD_DIM = 64
NA_BAND_ROWS = 8
NA_WIN_COLS = 16
NA_TRIP_PAIRS = 8
NA_PAIR_BAND = 10
SUBLANES_BF16 = 16
DIFF_SLOTS = 3
LOG2E = math.log2(math.e)
VMEM_LIMIT = 48 * 1024 * 1024


def _cparams(sem):
    return pltpu.CompilerParams(dimension_semantics=sem, vmem_limit_bytes=VMEM_LIMIT)


def _mod_kernel(c_ref, w_ref, b_ref, o_ref):
    c = c_ref[...]
    s = c * jax.nn.sigmoid(c)
    o_ref[0] = jnp.dot(s, w_ref[0], preferred_element_type=F32) + b_ref[0]


def _modulation(cond, w_mod, b_mod):
    depth, d, d3 = w_mod.shape
    rows = cond.shape[0]
    tn = 1024
    return pl.pallas_call(
        _mod_kernel,
        grid=(depth, d3 // tn),
        in_specs=[
            pl.BlockSpec((rows, d), lambda l, j: (0, 0)),
            pl.BlockSpec((1, d, tn), lambda l, j: (l, 0, j)),
            pl.BlockSpec((1, 1, tn), lambda l, j: (l, 0, j)),
        ],
        out_specs=pl.BlockSpec((1, rows, tn), lambda l, j: (l, 0, j)),
        out_shape=jax.ShapeDtypeStruct((depth, rows, d3), F32),
        compiler_params=_cparams(("parallel", "parallel")),
        name="modulation",
    )(cond, w_mod, b_mod.reshape(depth, 1, d3))


def _cast_kernel(w_ref, o_ref):
    o_ref[...] = w_ref[...].astype(o_ref.dtype)


def _to_bf16(wts, tn):
    depth, rows, cols = wts.shape
    spec = pl.BlockSpec((1, rows, tn), lambda l, j: (l, 0, j))
    return pl.pallas_call(
        _cast_kernel,
        grid=(depth, cols // tn),
        in_specs=[spec],
        out_specs=spec,
        out_shape=jax.ShapeDtypeStruct(wts.shape, BF16),
        compiler_params=_cparams(("parallel", "parallel")),
        name="weights_to_bf16",
    )(wts)


def _rope_store(t, cos, sin, is_first, o_ref):
    for cidx in range(t.shape[1] // LANES):
        tc = t[:, cidx * LANES:(cidx + 1) * LANES]
        partner = jnp.where(is_first, pltpu.roll(tc, LANES - 16, axis=1), pltpu.roll(tc, 16, axis=1))
        o_ref[0, :, cidx * LANES:(cidx + 1) * LANES] = (tc * cos + partner * sin).astype(o_ref.dtype)


def _outproj_value(y_ref, z_ref, x_ref, g_ref, w_ref, lng_ref, lnb_ref, alpha):
    z = z_ref[0].astype(F32)
    u = (y_ref[0].astype(F32) * (z * jax.nn.sigmoid(z))).astype(BF16)
    out = jnp.dot(u, w_ref[...], preferred_element_type=F32)
    r = alpha * x_ref[0] + (1.0 + g_ref[0]) * out
    mu = jnp.mean(r, axis=1, keepdims=True)
    rc = r - mu
    var = jnp.mean(rc * rc, axis=1, keepdims=True)
    return rc * lax.rsqrt(var + LN_EPS) * lng_ref[...] + lnb_ref[...]


def _inproj_kernel(*refs, rope, with_qz, q_scale, alpha):
    it = iter(refs)
    prev = [next(it) for _ in range(7)] if alpha is not None else None
    x_ref = prev[2] if prev else next(it)
    sc_ref, sh_ref = next(it), next(it)
    wq_ref = next(it) if with_qz else None
    wk_ref, wv_ref = next(it), next(it)
    wz_ref = next(it) if with_qz else None
    cos_ref, sin_ref = (next(it), next(it)) if rope else (None, None)
    xo_ref = next(it) if prev else None
    q_ref = next(it) if with_qz else None
    k_ref, v_ref = next(it), next(it)
    z_ref = next(it) if with_qz else None

    if prev:
        x_val = _outproj_value(*prev, alpha)
        xo_ref[0] = x_val
    else:
        x_val = x_ref[0]
    h = (x_val * (1.0 + sc_ref[0]) + sh_ref[0]).astype(BF16)
    if rope:
        cos, sin = cos_ref[...], sin_ref[...]
        lane = lax.broadcasted_iota(jnp.int32, cos.shape, 1)
        is_first = (lane // 16) % 2 == 0
    if with_qz:
        q = jnp.dot(h, wq_ref[...], preferred_element_type=F32) * q_scale
        if rope:
            _rope_store(q, cos, sin, is_first, q_ref)
        else:
            q_ref[0] = q.astype(q_ref.dtype)
    k = jnp.dot(h, wk_ref[...], preferred_element_type=F32)
    if rope:
        _rope_store(k, cos, sin, is_first, k_ref)
    else:
        k_ref[0] = k.astype(k_ref.dtype)
    v_ref[0] = lax.dot_general(wv_ref[...], h, (((1,), (1,)), ((), ())),
                               preferred_element_type=F32).astype(v_ref.dtype)
    if with_qz:
        z_ref[0] = jnp.dot(h, wz_ref[...], preferred_element_type=F32).astype(z_ref.dtype)


def _inproj(x, sc, sh, w_all, layer, wv_t, *, tm, rope_tables, with_qz, q_scale, prev=None):
    b, n, d = x.shape
    w = w_all.shape[2] // 4
    per_batch = sc.shape[0] == b
    mod_map = (lambda i, t: (i, 0, 0)) if per_batch else (lambda i, t: (0, 0, 0))
    const = lambda i, t: (0, 0)
    rope = rope_tables is not None
    once = dict(pipeline_mode=pl.Buffered(1))
    wspec = lambda col: pl.BlockSpec((None, d, w), lambda i, t: (layer, 0, col), **once)
    row = lambda i, t: (i, t, 0)
    row_spec = pl.BlockSpec((1, tm, w), row)
    row_shape = jax.ShapeDtypeStruct((b, n, w), BF16)

    out_shape, out_specs = [], []
    if prev is None:
        args, in_specs, alpha = [x], [pl.BlockSpec((1, tm, d), row)], None
    else:
        y_prev, z_prev, gate, w_out_all, ln_g_all, ln_b_all, prev_layer, alpha = prev
        lay = lambda i, t: (prev_layer, 0, 0)
        args = [y_prev, z_prev, x, gate, w_out_all, ln_g_all, ln_b_all]
        in_specs = [row_spec, row_spec, pl.BlockSpec((1, tm, d), row), pl.BlockSpec((1, 1, d), mod_map),
                    pl.BlockSpec((None, w, d), lay, **once), pl.BlockSpec((None, 1, d), lay),
                    pl.BlockSpec((None, 1, d), lay)]
        out_shape.append(jax.ShapeDtypeStruct((b, n, d), F32))
        out_specs.append(pl.BlockSpec((1, tm, d), row))
    args += [sc, sh]
    in_specs += [pl.BlockSpec((1, 1, d), mod_map), pl.BlockSpec((1, 1, d), mod_map)]
    if with_qz:
        args.append(w_all); in_specs.append(wspec(0))
        out_shape.append(row_shape); out_specs.append(row_spec)
    args += [w_all, wv_t]
    in_specs += [wspec(1), pl.BlockSpec(wv_t.shape, const, **once)]
    out_shape.append(row_shape); out_specs.append(row_spec)
    out_shape.append(jax.ShapeDtypeStruct((b, w, n), BF16))
    out_specs.append(pl.BlockSpec((1, w, tm), lambda i, t: (i, 0, t)))
    if with_qz:
        args.append(w_all); in_specs.append(wspec(3))
        out_shape.append(row_shape); out_specs.append(row_spec)
    if rope:
        args += list(rope_tables)
        in_specs += [pl.BlockSpec((tm, LANES), lambda i, t: (t, 0))] * 2

    outs = pl.pallas_call(
        functools.partial(_inproj_kernel, rope=rope, with_qz=with_qz, q_scale=q_scale, alpha=alpha),
        grid=(b, n // tm),
        in_specs=in_specs,
        out_specs=out_specs,
        out_shape=out_shape,
        compiler_params=_cparams(("parallel", "parallel")),
        name="inproj",
    )(*args)
    outs = list(outs)
    x_new = [outs.pop(0)] if prev is not None else []
    if with_qz:
        return (*x_new, *outs)
    return (*x_new, None, outs[0], outs[1], None)


def _stack_heads(q, first_head):
    zero = jnp.zeros_like(q)
    return jnp.concatenate([jnp.where(first_head, q, zero), jnp.where(first_head, zero, q)], axis=0)


_NT = (((1,), (1,)), ((), ()))


def _na_kernel(q_ref, k_ref, vt_ref, kc_ref, vct_ref, bias_ref, o_ref, s_ref, *, rows):
    kc = kc_ref[0]
    n_ctx = kc.shape[0]
    first_head = lax.broadcasted_iota(jnp.int32, (GRID_W, LANES), 1) < HEAD_DIM
    band = NA_PAIR_BAND * GRID_W
    neg_tile = bias_ref.shape[1] - 1
    n_pairs = rows // 2
    assert rows % 2 == 0 and n_pairs >= 2 * NA_TRIP_PAIRS

    def band_row(pair):
        return jnp.clip(2 * pair - NA_BAND_ROWS // 2, 0, rows - NA_PAIR_BAND)

    def stage_a(pair, slot):
        r = 2 * pair
        b0 = band_row(pair)
        q2 = q_ref[0, pl.ds(pl.multiple_of(r * GRID_W, 2 * GRID_W), 2 * GRID_W), :]
        qcat = jnp.concatenate([_stack_heads(q2[:GRID_W], first_head),
                                _stack_heads(q2[GRID_W:], first_head)], axis=0)
        kband = k_ref[0, pl.ds(pl.multiple_of(b0 * GRID_W, 2 * GRID_W), band), :]
        tiles = []
        for i in range(NA_PAIR_BAND):
            per_row = []
            for rho in range(2):
                start = jnp.clip(r + rho - NA_BAND_ROWS // 2, 0, rows - NA_BAND_ROWS)
                off = b0 + i - start
                dr = b0 + i - (r + rho) + NA_BAND_ROWS - 1
                idx = jnp.where((off >= 0) & (off < NA_BAND_ROWS), dr, neg_tile)
                per_row.append(bias_ref[0, idx])
            tiles.append(jnp.concatenate(per_row, axis=1))
        st = lax.dot_general(kband, qcat, _NT, preferred_element_type=F32) + jnp.concatenate(tiles, axis=0)
        sc = lax.dot_general(kc, qcat, _NT, preferred_element_type=F32)
        s_ref[slot, 0:band, :] = st
        s_ref[slot, band:, :] = sc
        return jnp.maximum(jnp.max(st, axis=0, keepdims=True), jnp.max(sc, axis=0, keepdims=True))

    def stage_b(pair, slot, m):
        r = 2 * pair
        b0 = band_row(pair)
        p = jnp.exp2(s_ref[slot] - m).astype(BF16)
        vband = vt_ref[0, :, pl.ds(pl.multiple_of(b0 * GRID_W, 2 * GRID_W), band)]
        vaug = jnp.concatenate([vband, vct_ref[0]], axis=1)
        vaug = jnp.concatenate([vaug, jnp.ones((SUBLANES_BF16, band + n_ctx), BF16)], axis=0)
        acc = jnp.dot(vaug, p, preferred_element_type=F32)
        ot = (acc[0:LANES] / acc[LANES:LANES + 1]).T
        y = jnp.concatenate(
            [jnp.where(first_head, ot[rho * LANES:rho * LANES + GRID_W], ot[rho * LANES + GRID_W:(rho + 1) * LANES])
             for rho in range(2)], axis=0)
        o_ref[0, pl.ds(pl.multiple_of(r * GRID_W, 2 * GRID_W), 2 * GRID_W), :] = y.astype(o_ref.dtype)

    trip = NA_TRIP_PAIRS
    assert n_pairs % trip == 0 and trip % 4 == 0
    ms = [stage_a(0, 0), stage_a(1, 1)]

    def steps(j0, count, ms, last_trip):
        for u in range(count):
            stage_b(j0 + u, u % 4, ms[0])
            nxt = [stage_a(j0 + u + 2, (u + 2) % 4)] if (not last_trip or u + 2 < count) else []
            ms = ms[1:] + nxt
        return ms

    def body(i, carry):
        return tuple(steps(trip * i, trip, list(carry), False))

    ms = list(lax.fori_loop(0, n_pairs // trip - 1, body, tuple(ms)))
    steps(n_pairs - trip, trip, ms, True)


def _na_attention(q, k, vt, kc, vct, bias):
    b, n, w = q.shape
    c = kc.shape[1]
    rows = n // GRID_W
    hp = w // LANES
    tok = lambda i, j: (i, 0, j)
    chan = lambda i, j: (i, j, 0)
    return pl.pallas_call(
        functools.partial(_na_kernel, rows=rows),
        grid=(b, hp),
        in_specs=[
            pl.BlockSpec((1, n, LANES), tok), pl.BlockSpec((1, n, LANES), tok),
            pl.BlockSpec((1, LANES, n), chan),
            pl.BlockSpec((1, c, LANES), tok), pl.BlockSpec((1, LANES, c), chan),
            pl.BlockSpec((1,) + bias.shape[1:], lambda i, j: (j, 0, 0, 0)),
        ],
        out_specs=pl.BlockSpec((1, n, LANES), tok),
        out_shape=jax.ShapeDtypeStruct((b, n, w), BF16),
        scratch_shapes=[pltpu.VMEM((4, NA_PAIR_BAND * GRID_W + c, 2 * LANES), F32)],
        compiler_params=_cparams(("parallel", "parallel")),
        name="na_attention",
    )(q, k, vt, kc, vct, bias)


def _na_ctx_kernel(q_ref, kc_ref, vct_ref, o_ref):
    n = q_ref.shape[1]
    first_head = lax.broadcasted_iota(jnp.int32, (n, LANES), 1) < HEAD_DIM
    for hp in range(q_ref.shape[2] // LANES):
        lanes = slice(hp * LANES, (hp + 1) * LANES)
        s = lax.dot_general(_stack_heads(q_ref[0, :, lanes], first_head), kc_ref[0, :, lanes], _NT,
                            preferred_element_type=F32)
        m = jnp.max(s, axis=1, keepdims=True)
        p = jnp.exp2(s - m)
        o2 = (lax.dot_general(p.astype(BF16), vct_ref[0, lanes, :], _NT, preferred_element_type=F32)
              / jnp.sum(p, axis=1, keepdims=True))
        o_ref[0, :, lanes] = jnp.where(first_head, o2[:n], o2[n:]).astype(o_ref.dtype)


def _na_ctx_attention(qc, kc, vct):
    b, c, w = qc.shape
    tok = pl.BlockSpec((1, c, w), lambda i: (i, 0, 0))
    return pl.pallas_call(
        _na_ctx_kernel,
        grid=(b,),
        in_specs=[tok, tok, pl.BlockSpec((1, w, c), lambda i: (i, 0, 0))],
        out_specs=tok,
        out_shape=jax.ShapeDtypeStruct((b, c, w), BF16),
        compiler_params=_cparams(("parallel",)),
        name="na_ctx_attention",
    )(qc, kc, vct)


def _na_bias_table(rpb):
    h, n_dr, n_rel = rpb.shape
    kj = jnp.arange(GRID_W)[:, None]
    lane = jnp.arange(2 * GRID_W)[None, :]
    hd, wq = lane // GRID_W, lane % GRID_W
    col_start = jnp.clip(wq - NA_WIN_COLS // 2, 0, GRID_W - NA_WIN_COLS)
    valid = (kj >= col_start) & (kj < col_start + NA_WIN_COLS)
    pad = GRID_W - NA_WIN_COLS
    rev = jnp.pad(rpb, ((0, 0), (0, 0), (pad + 1, pad)))[:, :, ::-1]
    skew = jnp.broadcast_to(rev[:, :, None, :], (h, n_dr, GRID_W, 2 * GRID_W))
    skew = skew.reshape(h, n_dr, 2 * GRID_W * GRID_W)[:, :, :GRID_W * (2 * GRID_W - 1)]
    skew = skew.reshape(h, n_dr, GRID_W, 2 * GRID_W - 1)[:, :, :, GRID_W - 1:]
    skew = skew.reshape(h // 2, 2, n_dr, GRID_W, GRID_W)
    t = jnp.where(valid[None, None], jnp.concatenate([skew[:, 0], skew[:, 1]], axis=-1), NEG)
    neg = jnp.full((h // 2, 1, GRID_W, 2 * GRID_W), NEG, F32)
    return jnp.concatenate([t.astype(F32), neg], axis=1)


def _diff_kernel(*refs, n_lat, kb, tq, lambda_init):
    it = iter(refs)
    q_ref = next(it)
    k_ref, vt_ref = (next(it), next(it)) if n_lat else (None, None)
    kc_ref, vct_ref, lam_ref, g_ref, o_ref, acc_ref, s_ref = (next(it) for _ in range(7))
    n_ctx = kc_ref.shape[1]
    n_tiles = q_ref.shape[1] // tq
    first_map = lax.broadcasted_iota(jnp.int32, (tq, LANES), 1) < HEAD_DIM
    lp = lam_ref[...]
    lam = (jnp.exp(jnp.sum(lp[0:1] * lp[1:2], axis=1, keepdims=True))
           - jnp.exp(jnp.sum(lp[2:3] * lp[3:4], axis=1, keepdims=True)) + lambda_init)

    def kblock(j):
        return k_ref[0, pl.ds(pl.multiple_of(j * kb, kb), kb), :]

    def vblock(j):
        return vt_ref[0, :, pl.ds(pl.multiple_of(j * kb, kb), kb)]

    for t in range(n_tiles):
        rows = slice(t * tq, (t + 1) * tq)
        qcat = _stack_heads(q_ref[0, rows, :], first_map)
        acc_ref[t] = jnp.zeros(acc_ref.shape[1:], F32)

        def scores(kblk, slot, t=t, qcat=qcat):
            st = lax.dot_general(kblk, qcat, _NT, preferred_element_type=F32)
            s_ref[DIFF_SLOTS * t + slot, 0:kblk.shape[0], :] = st
            return jnp.max(st, axis=0, keepdims=True)

        def consume(slot, vtblk, bmax, m_old, t=t):
            m_new = jnp.maximum(m_old, bmax)
            alpha = jnp.exp2(m_old - m_new)
            p = jnp.exp2(s_ref[DIFF_SLOTS * t + slot, 0:vtblk.shape[1], :] - m_new)
            vaug = jnp.concatenate([vtblk, jnp.ones((SUBLANES_BF16, vtblk.shape[1]), BF16)], axis=0)
            acc_ref[t] = alpha * acc_ref[t] + jnp.dot(vaug, p.astype(BF16), preferred_element_type=F32)
            return m_new

        m = jnp.full((1, 2 * tq), NEG, F32)
        if n_lat:
            nb = n_lat // kb
            assert nb >= 2 and (nb - 2) % DIFF_SLOTS == 0 and n_ctx <= kb
            bm = scores(kblock(0), 0)

            def body(i, carry, scores=scores, consume=consume):
                m, bm = carry
                for u in range(DIFF_SLOTS):
                    j = DIFF_SLOTS * i + u
                    bm_next = scores(kblock(j + 1), (u + 1) % DIFF_SLOTS)
                    m = consume(u, vblock(j), bm, m)
                    bm = bm_next
                return m, bm

            m, bm = lax.fori_loop(0, (nb - 2) // DIFF_SLOTS, body, (m, bm))
            bm_next = scores(kblock(nb - 1), (nb - 1) % DIFF_SLOTS)
            m = consume((nb - 2) % DIFF_SLOTS, vblock(nb - 2), bm, m)
            bm_ctx = scores(kc_ref[0], nb % DIFF_SLOTS)
            m = consume((nb - 1) % DIFF_SLOTS, vblock(nb - 1), bm_next, m)
            consume(nb % DIFF_SLOTS, vct_ref[0], bm_ctx, m)
        else:
            consume(0, vct_ref[0], scores(kc_ref[0], 0), m)

        o_all = acc_ref[t, 0:LANES, :] / acc_ref[t, LANES:LANES + 1, :]
        ot = o_all[:, :tq] - lam * o_all[:, tq:]
        ms = jnp.mean(ot * ot, axis=0, keepdims=True)
        ot = ot * lax.rsqrt(ms + LN_EPS)
        o_ref[0, rows, :] = (ot.T * g_ref[...] * (1.0 - lambda_init)).astype(o_ref.dtype)


def _diff_attention(q, k, vt, kc, vct, lam_params, subln_g, *, lambda_init, tq, kb, tiles=1):
    b, nq, w = q.shape
    c = kc.shape[1]
    heads = w // LANES
    n_lat = 0 if k is None else k.shape[1]
    step_rows = tiles * tq
    args = [q]
    in_specs = [pl.BlockSpec((1, step_rows, LANES), lambda i, h, t: (i, t, h))]
    if n_lat:
        args += [k, vt]
        in_specs += [pl.BlockSpec((1, n_lat, LANES), lambda i, h, t: (i, 0, h)),
                     pl.BlockSpec((1, LANES, n_lat), lambda i, h, t: (i, h, 0))]
    args += [kc, vct, lam_params, subln_g.reshape(1, LANES)]
    in_specs += [pl.BlockSpec((1, c, LANES), lambda i, h, t: (i, 0, h)),
                 pl.BlockSpec((1, LANES, c), lambda i, h, t: (i, h, 0)),
                 pl.BlockSpec(lam_params.shape, lambda i, h, t: (0, 0)),
                 pl.BlockSpec((1, LANES), lambda i, h, t: (0, 0))]
    return pl.pallas_call(
        functools.partial(_diff_kernel, n_lat=n_lat, kb=kb, tq=tq, lambda_init=lambda_init),
        grid=(b, heads, nq // step_rows),
        in_specs=in_specs,
        out_specs=pl.BlockSpec((1, step_rows, LANES), lambda i, h, t: (i, t, h)),
        out_shape=jax.ShapeDtypeStruct((b, nq, w), BF16),
        scratch_shapes=[pltpu.VMEM((tiles, LANES + SUBLANES_BF16, 2 * tq), F32),
                        pltpu.VMEM((DIFF_SLOTS * tiles, kb, 2 * tq), F32)],
        compiler_params=_cparams(("parallel", "parallel", "arbitrary")),
        name="diff_attention",
    )(*args)


def _outproj_kernel(y_ref, z_ref, x_ref, g_ref, w_ref, lng_ref, lnb_ref, o_ref, *, alpha):
    o_ref[0] = _outproj_value(y_ref, z_ref, x_ref, g_ref, w_ref, lng_ref, lnb_ref, alpha)


def _outproj(y, z, x, gate, w_out_all, ln_g_all, ln_b_all, layer, *, tm, alpha):
    b, n, d = x.shape
    w = y.shape[2]
    per_batch = gate.shape[0] == b
    mod_map = (lambda i, t: (i, 0, 0)) if per_batch else (lambda i, t: (0, 0, 0))
    row = lambda i, t: (i, t, 0)
    lay = lambda i, t: (layer, 0, 0)
    return pl.pallas_call(
        functools.partial(_outproj_kernel, alpha=alpha),
        grid=(b, n // tm),
        in_specs=[
            pl.BlockSpec((1, tm, w), row), pl.BlockSpec((1, tm, w), row),
            pl.BlockSpec((1, tm, d), row), pl.BlockSpec((1, 1, d), mod_map),
            pl.BlockSpec((None, w, d), lay), pl.BlockSpec((None, 1, d), lay), pl.BlockSpec((None, 1, d), lay),
        ],
        out_specs=pl.BlockSpec((1, tm, d), row),
        out_shape=jax.ShapeDtypeStruct((b, n, d), F32),
        compiler_params=_cparams(("parallel", "parallel")),
        name="outproj_ln",
    )(y, z, x, gate, w_out_all, ln_g_all, ln_b_all)


def _rope_tables(n_tok):
    t = jnp.arange(n_tok, dtype=jnp.int32)
    row = (t // GRID_W).astype(F32)[:, None]
    col = (t % GRID_W).astype(F32)[:, None]
    n_freq = HEAD_DIM // 4
    inv_freq = ROPE_BASE ** (-jnp.arange(n_freq, dtype=F32) / n_freq)
    lane = jnp.arange(LANES)
    pos = jnp.where(((lane % HEAD_DIM) // (HEAD_DIM // 2) == 0)[None, :], row, col)
    ang = pos * inv_freq[lane % n_freq][None, :]
    first = ((lane // n_freq) % 2 == 0)[None, :]
    return jnp.cos(ang), jnp.where(first, -jnp.sin(ang), jnp.sin(ang))


def kernel(x, c, ctx, c_ctx, w_mod, b_mod, w_in, w_out, ln_g, ln_b, na_rpb, diff_lambda, diff_subln_g):
    depth, d, _ = w_mod.shape
    b, n, _ = x.shape
    w = w_out.shape[1]
    n_ctx = ctx.shape[1]
    alpha = (2.0 * depth) ** 0.25
    q_scale = HEAD_DIM ** -0.5 * LOG2E

    cond = jnp.concatenate([c, c_ctx[None, :], jnp.zeros((16 - b - 1, d), F32)], axis=0)
    mods = _modulation(cond, w_mod, b_mod)
    w_in_bf = _to_bf16(w_in, 1024)
    w_out_bf = _to_bf16(w_out, 1024)
    ln_g3, ln_b3 = ln_g[:, None, :], ln_b[:, None, :]
    rope_tables = _rope_tables(n)
    wv_t_all = jnp.swapaxes(w_in[:, :, 2 * w:3 * w], 1, 2).astype(BF16)

    prev = prev_c = None
    for l in range(depth):
        need_ctx = l < depth - 1
        is_diff = l % 2 == 1
        m = mods[l]
        sh, sc, g = (m[:b, None, i * d:(i + 1) * d] for i in range(3))
        shc, scc, gc = (m[b:b + 1, None, i * d:(i + 1) * d] for i in range(3))
        wv_t = wv_t_all[l]
        res = _inproj(x, sc, sh, w_in_bf, l, wv_t, tm=512, rope_tables=rope_tables if is_diff else None,
                      with_qz=True, q_scale=q_scale, prev=prev)
        if prev is not None:
            x, res = res[0], res[1:]
        q, k, v, z = res
        res = _inproj(ctx, scc, shc, w_in_bf, l, wv_t, tm=n_ctx, rope_tables=None,
                      with_qz=need_ctx, q_scale=q_scale, prev=prev_c)
        if prev_c is not None:
            ctx, res = res[0], res[1:]
        qc, kc, vc, zc = res
        if is_diff:
            lam_p, sub_g = diff_lambda[l // 2], diff_subln_g[l // 2]
            lambda_init = 0.8 - 0.6 * math.exp(-0.3 * l)
            y = _diff_attention(q, k, v, kc, vc, lam_p, sub_g, lambda_init=lambda_init, tq=512, kb=512, tiles=4)
            if need_ctx:
                yc = _diff_attention(qc, None, None, kc, vc, lam_p, sub_g,
                                     lambda_init=lambda_init, tq=n_ctx, kb=512)
        else:
            y = _na_attention(q, k, v, kc, vc, _na_bias_table(na_rpb[l // 2] * LOG2E))
            if need_ctx:
                yc = _na_ctx_attention(qc, kc, vc)
        prev = (y, z, g, w_out_bf, ln_g3, ln_b3, l, alpha)
        prev_c = (yc, zc, gc, w_out_bf, ln_g3, ln_b3, l, alpha) if need_ctx else None
    y, z, g, w_out_all, ln_g_all, ln_b_all, last = prev[:7]
    return _outproj(y, z, x, g, w_out_all, ln_g_all, ln_b_all, last, tm=512, alpha=alpha)
```

```python
import functools
import math

import jax
import jax.numpy as jnp
from jax import lax
from jax.experimental import pallas as pl
from jax.experimental.pallas import tpu as pltpu

F32 = jnp.float32
BF16 = jnp.bfloat16

GRID_W = 64
ROPE_BASE = 10000.0
LN_EPS = 1e-5
NEG = -1e30
LANES = 128
HEAD_DIM = 64
NA_BAND_ROWS = 8
NA_WIN_COLS = 16
NA_TRIP_PAIRS = 8
NA_PAIR_BAND = 10
SUBLANES_BF16 = 16
DIFF_SLOTS = 3
LOG2E = math.log2(math.e)
VMEM_LIMIT = 48 * 1024 * 1024


def _cparams(sem):
    return pltpu.CompilerParams(dimension_semantics=sem, vmem_limit_bytes=VMEM_LIMIT)


def _mod_kernel(c_ref, w_ref, b_ref, o_ref):
    c = c_ref[...]
    s = c * jax.nn.sigmoid(c)
    o_ref[0] = jnp.dot(s, w_ref[0], preferred_element_type=F32) + b_ref[0]


def _modulation(cond, w_mod, b_mod):
    depth, d, d3 = w_mod.shape
    rows = cond.shape[0]
    tn = 1024
    return pl.pallas_call(
        _mod_kernel,
        grid=(depth, d3 // tn),
        in_specs=[
            pl.BlockSpec((rows, d), lambda l, j: (0, 0)),
            pl.BlockSpec((1, d, tn), lambda l, j: (l, 0, j)),
            pl.BlockSpec((1, 1, tn), lambda l, j: (l, 0, j)),
        ],
        out_specs=pl.BlockSpec((1, rows, tn), lambda l, j: (l, 0, j)),
        out_shape=jax.ShapeDtypeStruct((depth, rows, d3), F32),
        compiler_params=_cparams(("parallel", "parallel")),
        name="modulation",
    )(cond, w_mod, b_mod.reshape(depth, 1, d3))


def _cast_kernel(w_ref, o_ref):
    o_ref[...] = w_ref[...].astype(o_ref.dtype)


def _to_bf16(wts, tn):
    depth, rows, cols = wts.shape
    spec = pl.BlockSpec((1, rows, tn), lambda l, j: (l, 0, j))
    return pl.pallas_call(
        _cast_kernel,
        grid=(depth, cols // tn),
        in_specs=[spec],
        out_specs=spec,
        out_shape=jax.ShapeDtypeStruct(wts.shape, BF16),
        compiler_params=_cparams(("parallel", "parallel")),
        name="weights_to_bf16",
    )(wts)


def _rope_store(t, cos, sin, is_first, o_ref):
    for cidx in range(t.shape[1] // LANES):
        tc = t[:, cidx * LANES:(cidx + 1) * LANES]
        partner = jnp.where(is_first, pltpu.roll(tc, LANES - 16, axis=1), pltpu.roll(tc, 16, axis=1))
        o_ref[0, :, cidx * LANES:(cidx + 1) * LANES] = (tc * cos + partner * sin).astype(o_ref.dtype)


def _outproj_value(y_ref, z_ref, x_ref, g_ref, w_ref, lng_ref, lnb_ref, alpha):
    z = z_ref[0].astype(F32)
    u = (y_ref[0].astype(F32) * (z * jax.nn.sigmoid(z))).astype(BF16)
    out = jnp.dot(u, w_ref[...], preferred_element_type=F32)
    r = alpha * x_ref[0] + (1.0 + g_ref[0]) * out
    mu = jnp.mean(r, axis=1, keepdims=True)
    rc = r - mu
    var = jnp.mean(rc * rc, axis=1, keepdims=True)
    return rc * lax.rsqrt(var + LN_EPS) * lng_ref[...] + lnb_ref[...]


def _inproj_kernel(*refs, rope, with_qz, q_scale, alpha):
    it = iter(refs)
    prev = [next(it) for _ in range(7)] if alpha is not None else None
    x_ref = prev[2] if prev else next(it)
    sc_ref, sh_ref = next(it), next(it)
    wq_ref = next(it) if with_qz else None
    wk_ref, wv_ref = next(it), next(it)
    wz_ref = next(it) if with_qz else None
    cos_ref, sin_ref = (next(it), next(it)) if rope else (None, None)
    xo_ref = next(it) if prev else None
    q_ref = next(it) if with_qz else None
    k_ref, v_ref = next(it), next(it)
    z_ref = next(it) if with_qz else None

    if prev:
        x_val = _outproj_value(*prev, alpha)
        xo_ref[0] = x_val
    else:
        x_val = x_ref[0]
    h = (x_val * (1.0 + sc_ref[0]) + sh_ref[0]).astype(BF16)
    if rope:
        cos, sin = cos_ref[...], sin_ref[...]
        lane = lax.broadcasted_iota(jnp.int32, cos.shape, 1)
        is_first = (lane // 16) % 2 == 0
    if with_qz:
        q = jnp.dot(h, wq_ref[...], preferred_element_type=F32) * q_scale
        if rope:
            _rope_store(q, cos, sin, is_first, q_ref)
        else:
            q_ref[0] = q.astype(q_ref.dtype)
    k = jnp.dot(h, wk_ref[...], preferred_element_type=F32)
    if rope:
        _rope_store(k, cos, sin, is_first, k_ref)
    else:
        k_ref[0] = k.astype(k_ref.dtype)
    v_ref[0] = lax.dot_general(wv_ref[...], h, (((1,), (1,)), ((), ())),
                               preferred_element_type=F32).astype(v_ref.dtype)
    if with_qz:
        z_ref[0] = jnp.dot(h, wz_ref[...], preferred_element_type=F32).astype(z_ref.dtype)


def _inproj(x, sc, sh, w_all, layer, wv_t, *, tm, rope_tables, with_qz, q_scale, prev=None):
    b, n, d = x.shape
    w = w_all.shape[2] // 4
    per_batch = sc.shape[0] == b
    mod_map = (lambda i, t: (i, 0, 0)) if per_batch else (lambda i, t: (0, 0, 0))
    const = lambda i, t: (0, 0)
    rope = rope_tables is not None
    once = dict(pipeline_mode=pl.Buffered(1))
    wspec = lambda col: pl.BlockSpec((None, d, w), lambda i, t: (layer, 0, col), **once)
    row = lambda i, t: (i, t, 0)
    row_spec = pl.BlockSpec((1, tm, w), row)
    row_shape = jax.ShapeDtypeStruct((b, n, w), BF16)

    out_shape, out_specs = [], []
    if prev is None:
        args, in_specs, alpha = [x], [pl.BlockSpec((1, tm, d), row)], None
    else:
        y_prev, z_prev, gate, w_out_all, ln_g_all, ln_b_all, prev_layer, alpha = prev
        lay = lambda i, t: (prev_layer, 0, 0)
        args = [y_prev, z_prev, x, gate, w_out_all, ln_g_all, ln_b_all]
        in_specs = [row_spec, row_spec, pl.BlockSpec((1, tm, d), row), pl.BlockSpec((1, 1, d), mod_map),
                    pl.BlockSpec((None, w, d), lay, **once), pl.BlockSpec((None, 1, d), lay),
                    pl.BlockSpec((None, 1, d), lay)]
        out_shape.append(jax.ShapeDtypeStruct((b, n, d), F32))
        out_specs.append(pl.BlockSpec((1, tm, d), row))
    args += [sc, sh]
    in_specs += [pl.BlockSpec((1, 1, d), mod_map), pl.BlockSpec((1, 1, d), mod_map)]
    if with_qz:
        args.append(w_all); in_specs.append(wspec(0))
        out_shape.append(row_shape); out_specs.append(row_spec)
    args += [w_all, wv_t]
    in_specs += [wspec(1), pl.BlockSpec(wv_t.shape, const, **once)]
    out_shape.append(row_shape); out_specs.append(row_spec)
    out_shape.append(jax.ShapeDtypeStruct((b, w, n), BF16))
    out_specs.append(pl.BlockSpec((1, w, tm), lambda i, t: (i, 0, t)))
    if with_qz:
        args.append(w_all); in_specs.append(wspec(3))
        out_shape.append(row_shape); out_specs.append(row_spec)
    if rope:
        args += list(rope_tables)
        in_specs += [pl.BlockSpec((tm, LANES), lambda i, t: (t, 0))] * 2

    outs = pl.pallas_call(
        functools.partial(_inproj_kernel, rope=rope, with_qz=with_qz, q_scale=q_scale, alpha=alpha),
        grid=(b, n // tm),
        in_specs=in_specs,
        out_specs=out_specs,
        out_shape=out_shape,
        compiler_params=_cparams(("parallel", "parallel")),
        name="inproj",
    )(*args)
    outs = list(outs)
    x_new = [outs.pop(0)] if prev is not None else []
    if with_qz:
        return (*x_new, *outs)
    return (*x_new, None, outs[0], outs[1], None)


def _stack_heads(q, first_head):
    zero = jnp.zeros_like(q)
    return jnp.concatenate([jnp.where(first_head, q, zero), jnp.where(first_head, zero, q)], axis=0)


_NT = (((1,), (1,)), ((), ()))


def _na_kernel(q_ref, k_ref, vt_ref, kc_ref, vct_ref, bias_ref, o_ref, s_ref, *, rows):
    kc = kc_ref[0]
    n_ctx = kc.shape[0]
    first_head = lax.broadcasted_iota(jnp.int32, (GRID_W, LANES), 1) < HEAD_DIM
    band = NA_PAIR_BAND * GRID_W
    neg_tile = bias_ref.shape[1] - 1
    n_pairs = rows // 2
    assert rows % 2 == 0 and n_pairs >= 2 * NA_TRIP_PAIRS

    def band_row(pair):
        return jnp.clip(2 * pair - NA_BAND_ROWS // 2, 0, rows - NA_PAIR_BAND)

    def stage_a(pair, slot):
        r = 2 * pair
        b0 = band_row(pair)
        q2 = q_ref[0, pl.ds(pl.multiple_of(r * GRID_W, 2 * GRID_W), 2 * GRID_W), :]
        qcat = jnp.concatenate([_stack_heads(q2[:GRID_W], first_head),
                                _stack_heads(q2[GRID_W:], first_head)], axis=0)
        kband = k_ref[0, pl.ds(pl.multiple_of(b0 * GRID_W, 2 * GRID_W), band), :]
        tiles = []
        for i in range(NA_PAIR_BAND):
            per_row = []
            for rho in range(2):
                start = jnp.clip(r + rho - NA_BAND_ROWS // 2, 0, rows - NA_BAND_ROWS)
                off = b0 + i - start
                dr = b0 + i - (r + rho) + NA_BAND_ROWS - 1
                idx = jnp.where((off >= 0) & (off < NA_BAND_ROWS), dr, neg_tile)
                per_row.append(bias_ref[0, idx])
            tiles.append(jnp.concatenate(per_row, axis=1))
        st = lax.dot_general(kband, qcat, _NT, preferred_element_type=F32) + jnp.concatenate(tiles, axis=0)
        sc = lax.dot_general(kc, qcat, _NT, preferred_element_type=F32)
        s_ref[slot, 0:band, :] = st
        s_ref[slot, band:, :] = sc
        return jnp.maximum(jnp.max(st, axis=0, keepdims=True), jnp.max(sc, axis=0, keepdims=True))

    def stage_b(pair, slot, m):
        r = 2 * pair
        b0 = band_row(pair)
        p = jnp.exp2(s_ref[slot] - m).astype(BF16)
        vband = vt_ref[0, :, pl.ds(pl.multiple_of(b0 * GRID_W, 2 * GRID_W), band)]
        vaug = jnp.concatenate([vband, vct_ref[0]], axis=1)
        vaug = jnp.concatenate([vaug, jnp.ones((SUBLANES_BF16, band + n_ctx), BF16)], axis=0)
        acc = jnp.dot(vaug, p, preferred_element_type=F32)
        ot = (acc[0:LANES] / acc[LANES:LANES + 1]).T
        y = jnp.concatenate(
            [jnp.where(first_head, ot[rho * LANES:rho * LANES + GRID_W], ot[rho * LANES + GRID_W:(rho + 1) * LANES])
             for rho in range(2)], axis=0)
        o_ref[0, pl.ds(pl.multiple_of(r * GRID_W, 2 * GRID_W), 2 * GRID_W), :] = y.astype(o_ref.dtype)

    trip = NA_TRIP_PAIRS
    assert n_pairs % trip == 0 and trip % 4 == 0
    ms = [stage_a(0, 0), stage_a(1, 1)]

    def steps(j0, count, ms, last_trip):
        for u in range(count):
            stage_b(j0 + u, u % 4, ms[0])
            nxt = [stage_a(j0 + u + 2, (u + 2) % 4)] if (not last_trip or u + 2 < count) else []
            ms = ms[1:] + nxt
        return ms

    def body(i, carry):
        return tuple(steps(trip * i, trip, list(carry), False))

    ms = list(lax.fori_loop(0, n_pairs // trip - 1, body, tuple(ms)))
    steps(n_pairs - trip, trip, ms, True)


def _na_attention(q, k, vt, kc, vct, bias):
    b, n, w = q.shape
    c = kc.shape[1]
    rows = n // GRID_W
    hp = w // LANES
    tok = lambda i, j: (i, 0, j)
    chan = lambda i, j: (i, j, 0)
    return pl.pallas_call(
        functools.partial(_na_kernel, rows=rows),
        grid=(b, hp),
        in_specs=[
            pl.BlockSpec((1, n, LANES), tok), pl.BlockSpec((1, n, LANES), tok),
            pl.BlockSpec((1, LANES, n), chan),
            pl.BlockSpec((1, c, LANES), tok), pl.BlockSpec((1, LANES, c), chan),
            pl.BlockSpec((1,) + bias.shape[1:], lambda i, j: (j, 0, 0, 0)),
        ],
        out_specs=pl.BlockSpec((1, n, LANES), tok),
        out_shape=jax.ShapeDtypeStruct((b, n, w), BF16),
        scratch_shapes=[pltpu.VMEM((4, NA_PAIR_BAND * GRID_W + c, 2 * LANES), F32)],
        compiler_params=_cparams(("parallel", "parallel")),
        name="na_attention",
    )(q, k, vt, kc, vct, bias)


def _na_ctx_kernel(q_ref, kc_ref, vct_ref, o_ref):
    n = q_ref.shape[1]
    first_head = lax.broadcasted_iota(jnp.int32, (n, LANES), 1) < HEAD_DIM
    for hp in range(q_ref.shape[2] // LANES):
        lanes = slice(hp * LANES, (hp + 1) * LANES)
        s = lax.dot_general(_stack_heads(q_ref[0, :, lanes], first_head), kc_ref[0, :, lanes], _NT,
                            preferred_element_type=F32)
        m = jnp.max(s, axis=1, keepdims=True)
        p = jnp.exp2(s - m)
        o2 = (lax.dot_general(p.astype(BF16), vct_ref[0, lanes, :], _NT, preferred_element_type=F32)
              / jnp.sum(p, axis=1, keepdims=True))
        o_ref[0, :, lanes] = jnp.where(first_head, o2[:n], o2[n:]).astype(o_ref.dtype)


def _na_ctx_attention(qc, kc, vct):
    b, c, w = qc.shape
    tok = pl.BlockSpec((1, c, w), lambda i: (i, 0, 0))
    return pl.pallas_call(
        _na_ctx_kernel,
        grid=(b,),
        in_specs=[tok, tok, pl.BlockSpec((1, w, c), lambda i: (i, 0, 0))],
        out_specs=tok,
        out_shape=jax.ShapeDtypeStruct((b, c, w), BF16),
        compiler_params=_cparams(("parallel",)),
        name="na_ctx_attention",
    )(qc, kc, vct)


def _na_bias_table(rpb):
    h, n_dr, n_rel = rpb.shape
    kj = jnp.arange(GRID_W)[:, None]
    lane = jnp.arange(2 * GRID_W)[None, :]
    hd, wq = lane // GRID_W, lane % GRID_W
    col_start = jnp.clip(wq - NA_WIN_COLS // 2, 0, GRID_W - NA_WIN_COLS)
    valid = (kj >= col_start) & (kj < col_start + NA_WIN_COLS)
    pad = GRID_W - NA_WIN_COLS
    rev = jnp.pad(rpb, ((0, 0), (0, 0), (pad + 1, pad)))[:, :, ::-1]
    skew = jnp.broadcast_to(rev[:, :, None, :], (h, n_dr, GRID_W, 2 * GRID_W))
    skew = skew.reshape(h, n_dr, 2 * GRID_W * GRID_W)[:, :, :GRID_W * (2 * GRID_W - 1)]
    skew = skew.reshape(h, n_dr, GRID_W, 2 * GRID_W - 1)[:, :, :, GRID_W - 1:]
    skew = skew.reshape(h // 2, 2, n_dr, GRID_W, GRID_W)
    t = jnp.where(valid[None, None], jnp.concatenate([skew[:, 0], skew[:, 1]], axis=-1), NEG)
    neg = jnp.full((h // 2, 1, GRID_W, 2 * GRID_W), NEG, F32)
    return jnp.concatenate([t.astype(F32), neg], axis=1)


def _diff_kernel(*refs, n_lat, kb, tq, lambda_init):
    it = iter(refs)
    q_ref = next(it)
    k_ref, vt_ref = (next(it), next(it)) if n_lat else (None, None)
    kc_ref, vct_ref, lam_ref, g_ref, o_ref, acc_ref, s_ref = (next(it) for _ in range(7))
    n_ctx = kc_ref.shape[1]
    n_tiles = q_ref.shape[1] // tq
    first_map = lax.broadcasted_iota(jnp.int32, (tq, LANES), 1) < HEAD_DIM
    lp = lam_ref[...]
    lam = (jnp.exp(jnp.sum(lp[0:1] * lp[1:2], axis=1, keepdims=True))
           - jnp.exp(jnp.sum(lp[2:3] * lp[3:4], axis=1, keepdims=True)) + lambda_init)

    def kblock(j):
        return k_ref[0, pl.ds(pl.multiple_of(j * kb, kb), kb), :]

    def vblock(j):
        return vt_ref[0, :, pl.ds(pl.multiple_of(j * kb, kb), kb)]

    for t in range(n_tiles):
        rows = slice(t * tq, (t + 1) * tq)
        qcat = _stack_heads(q_ref[0, rows, :], first_map)
        acc_ref[t] = jnp.zeros(acc_ref.shape[1:], F32)

        def scores(kblk, slot, t=t, qcat=qcat):
            st = lax.dot_general(kblk, qcat, _NT, preferred_element_type=F32)
            s_ref[DIFF_SLOTS * t + slot, 0:kblk.shape[0], :] = st
            return jnp.max(st, axis=0, keepdims=True)

        def consume(slot, vtblk, bmax, m_old, t=t):
            m_new = jnp.maximum(m_old, bmax)
            alpha = jnp.exp2(m_old - m_new)
            p = jnp.exp2(s_ref[DIFF_SLOTS * t + slot, 0:vtblk.shape[1], :] - m_new)
            vaug = jnp.concatenate([vtblk, jnp.ones((SUBLANES_BF16, vtblk.shape[1]), BF16)], axis=0)
            acc_ref[t] = alpha * acc_ref[t] + jnp.dot(vaug, p.astype(BF16), preferred_element_type=F32)
            return m_new

        m = jnp.full((1, 2 * tq), NEG, F32)
        if n_lat:
            nb = n_lat // kb
            assert nb >= 2 and (nb - 2) % DIFF_SLOTS == 0 and n_ctx <= kb
            bm = scores(kblock(0), 0)

            def body(i, carry, scores=scores, consume=consume):
                m, bm = carry
                for u in range(DIFF_SLOTS):
                    j = DIFF_SLOTS * i + u
                    bm_next = scores(kblock(j + 1), (u + 1) % DIFF_SLOTS)
                    m = consume(u, vblock(j), bm, m)
                    bm = bm_next
                return m, bm

            m, bm = lax.fori_loop(0, (nb - 2) // DIFF_SLOTS, body, (m, bm))
            bm_next = scores(kblock(nb - 1), (nb - 1) % DIFF_SLOTS)
            m = consume((nb - 2) % DIFF_SLOTS, vblock(nb - 2), bm, m)
            bm_ctx = scores(kc_ref[0], nb % DIFF_SLOTS)
            m = consume((nb - 1) % DIFF_SLOTS, vblock(nb - 1), bm_next, m)
            consume(nb % DIFF_SLOTS, vct_ref[0], bm_ctx, m)
        else:
            consume(0, vct_ref[0], scores(kc_ref[0], 0), m)

        o_all = acc_ref[t, 0:LANES, :] / acc_ref[t, LANES:LANES + 1, :]
        ot = o_all[:, :tq] - lam * o_all[:, tq:]
        ms = jnp.mean(ot * ot, axis=0, keepdims=True)
        ot = ot * lax.rsqrt(ms + LN_EPS)
        o_ref[0, rows, :] = (ot.T * g_ref[...] * (1.0 - lambda_init)).astype(o_ref.dtype)


def _diff_attention(q, k, vt, kc, vct, lam_params, subln_g, *, lambda_init, tq, kb, tiles=1):
    b, nq, w = q.shape
    c = kc.shape[1]
    heads = w // LANES
    n_lat = 0 if k is None else k.shape[1]
    step_rows = tiles * tq
    args = [q]
    in_specs = [pl.BlockSpec((1, step_rows, LANES), lambda i, h, t: (i, t, h))]
    if n_lat:
        args += [k, vt]
        in_specs += [pl.BlockSpec((1, n_lat, LANES), lambda i, h, t: (i, 0, h)),
                     pl.BlockSpec((1, LANES, n_lat), lambda i, h, t: (i, h, 0))]
    args += [kc, vct, lam_params, subln_g.reshape(1, LANES)]
    in_specs += [pl.BlockSpec((1, c, LANES), lambda i, h, t: (i, 0, h)),
                 pl.BlockSpec((1, LANES, c), lambda i, h, t: (i, h, 0)),
                 pl.BlockSpec(lam_params.shape, lambda i, h, t: (0, 0)),
                 pl.BlockSpec((1, LANES), lambda i, h, t: (0, 0))]
    return pl.pallas_call(
        functools.partial(_diff_kernel, n_lat=n_lat, kb=kb, tq=tq, lambda_init=lambda_init),
        grid=(b, heads, nq // step_rows),
        in_specs=in_specs,
        out_specs=pl.BlockSpec((1, step_rows, LANES), lambda i, h, t: (i, t, h)),
        out_shape=jax.ShapeDtypeStruct((b, nq, w), BF16),
        scratch_shapes=[pltpu.VMEM((tiles, LANES + SUBLANES_BF16, 2 * tq), F32),
                        pltpu.VMEM((DIFF_SLOTS * tiles, kb, 2 * tq), F32)],
        compiler_params=_cparams(("parallel", "parallel", "arbitrary")),
        name="diff_attention",
    )(*args)


def _diff_ctx_kernel(q_ref, kc_ref, vct_ref, lam_ref, g_ref, o_ref, *, lambda_init):
    n = q_ref.shape[1]
    first_map = lax.broadcasted_iota(jnp.int32, (n, LANES), 1) < HEAD_DIM
    lp = lam_ref[...]
    lam = (jnp.exp(jnp.sum(lp[0:1] * lp[1:2], axis=1, keepdims=True))
           - jnp.exp(jnp.sum(lp[2:3] * lp[3:4], axis=1, keepdims=True)) + lambda_init)
    ones = jnp.ones((SUBLANES_BF16, kc_ref.shape[1]), BF16)
    for hh in range(q_ref.shape[2] // LANES):
        lanes = slice(hh * LANES, (hh + 1) * LANES)
        qcat = _stack_heads(q_ref[0, :, lanes], first_map)
        st = lax.dot_general(kc_ref[0, :, lanes], qcat, _NT, preferred_element_type=F32)
        p = jnp.exp2(st - jnp.max(st, axis=0, keepdims=True)).astype(BF16)
        acc = jnp.dot(jnp.concatenate([vct_ref[0, lanes, :], ones], axis=0), p, preferred_element_type=F32)
        o_all = acc[0:LANES] / acc[LANES:LANES + 1]
        ot = o_all[:, :n] - lam * o_all[:, n:]
        ot = ot * lax.rsqrt(jnp.mean(ot * ot, axis=0, keepdims=True) + LN_EPS)
        o_ref[0, :, lanes] = (ot.T * g_ref[...] * (1.0 - lambda_init)).astype(o_ref.dtype)


def _diff_ctx_attention(qc, kc, vct, lam_params, subln_g, *, lambda_init):
    b, c, w = qc.shape
    tok = pl.BlockSpec((1, c, w), lambda i: (i, 0, 0))
    return pl.pallas_call(
        functools.partial(_diff_ctx_kernel, lambda_init=lambda_init),
        grid=(b,),
        in_specs=[tok, tok, pl.BlockSpec((1, w, c), lambda i: (i, 0, 0)),
                  pl.BlockSpec(lam_params.shape, lambda i: (0, 0)), pl.BlockSpec((1, LANES), lambda i: (0, 0))],
        out_specs=tok,
        out_shape=jax.ShapeDtypeStruct((b, c, w), BF16),
        compiler_params=_cparams(("parallel",)),
        name="diff_ctx_attention",
    )(qc, kc, vct, lam_params, subln_g.reshape(1, LANES))


def _outproj_kernel(y_ref, z_ref, x_ref, g_ref, w_ref, lng_ref, lnb_ref, o_ref, *, alpha):
    o_ref[0] = _outproj_value(y_ref, z_ref, x_ref, g_ref, w_ref, lng_ref, lnb_ref, alpha)


def _outproj(y, z, x, gate, w_out_all, ln_g_all, ln_b_all, layer, *, tm, alpha):
    b, n, d = x.shape
    w = y.shape[2]
    per_batch = gate.shape[0] == b
    mod_map = (lambda i, t: (i, 0, 0)) if per_batch else (lambda i, t: (0, 0, 0))
    row = lambda i, t: (i, t, 0)
    lay = lambda i, t: (layer, 0, 0)
    return pl.pallas_call(
        functools.partial(_outproj_kernel, alpha=alpha),
        grid=(b, n // tm),
        in_specs=[
            pl.BlockSpec((1, tm, w), row), pl.BlockSpec((1, tm, w), row),
            pl.BlockSpec((1, tm, d), row), pl.BlockSpec((1, 1, d), mod_map),
            pl.BlockSpec((None, w, d), lay), pl.BlockSpec((None, 1, d), lay), pl.BlockSpec((None, 1, d), lay),
        ],
        out_specs=pl.BlockSpec((1, tm, d), row),
        out_shape=jax.ShapeDtypeStruct((b, n, d), F32),
        compiler_params=_cparams(("parallel", "parallel")),
        name="outproj_ln",
    )(y, z, x, gate, w_out_all, ln_g_all, ln_b_all)


def _rope_tables(n_tok):
    t = jnp.arange(n_tok, dtype=jnp.int32)
    row = (t // GRID_W).astype(F32)[:, None]
    col = (t % GRID_W).astype(F32)[:, None]
    n_freq = HEAD_DIM // 4
    inv_freq = ROPE_BASE ** (-jnp.arange(n_freq, dtype=F32) / n_freq)
    lane = jnp.arange(LANES)
    pos = jnp.where(((lane % HEAD_DIM) // (HEAD_DIM // 2) == 0)[None, :], row, col)
    ang = pos * inv_freq[lane % n_freq][None, :]
    first = ((lane // n_freq) % 2 == 0)[None, :]
    return jnp.cos(ang), jnp.where(first, -jnp.sin(ang), jnp.sin(ang))


def kernel(x, c, ctx, c_ctx, w_mod, b_mod, w_in, w_out, ln_g, ln_b, na_rpb, diff_lambda, diff_subln_g):
    depth, d, _ = w_mod.shape
    b, n, _ = x.shape
    w = w_out.shape[1]
    n_ctx = ctx.shape[1]
    alpha = (2.0 * depth) ** 0.25
    q_scale = HEAD_DIM ** -0.5 * LOG2E

    cond = jnp.concatenate([c, c_ctx[None, :], jnp.zeros((16 - b - 1, d), F32)], axis=0)
    mods = _modulation(cond, w_mod, b_mod)
    w_in_bf = _to_bf16(w_in, 1024)
    w_out_bf = _to_bf16(w_out, 1024)
    ln_g3, ln_b3 = ln_g[:, None, :], ln_b[:, None, :]
    rope_tables = _rope_tables(n)
    wv_t_all = jnp.swapaxes(w_in[:, :, 2 * w:3 * w], 1, 2).astype(BF16)

    prev = prev_c = None
    for l in range(depth):
        need_ctx = l < depth - 1
        is_diff = l % 2 == 1
        m = mods[l]
        sh, sc, g = (m[:b, None, i * d:(i + 1) * d] for i in range(3))
        shc, scc, gc = (m[b:b + 1, None, i * d:(i + 1) * d] for i in range(3))
        wv_t = wv_t_all[l]
        res = _inproj(x, sc, sh, w_in_bf, l, wv_t, tm=512, rope_tables=rope_tables if is_diff else None,
                      with_qz=True, q_scale=q_scale, prev=prev)
        if prev is not None:
            x, res = res[0], res[1:]
        q, k, v, z = res
        res = _inproj(ctx, scc, shc, w_in_bf, l, wv_t, tm=n_ctx, rope_tables=None,
                      with_qz=need_ctx, q_scale=q_scale, prev=prev_c)
        if prev_c is not None:
            ctx, res = res[0], res[1:]
        qc, kc, vc, zc = res
        if is_diff:
            lam_p, sub_g = diff_lambda[l // 2], diff_subln_g[l // 2]
            lambda_init = 0.8 - 0.6 * math.exp(-0.3 * l)
            y = _diff_attention(q, k, v, kc, vc, lam_p, sub_g, lambda_init=lambda_init, tq=512, kb=512, tiles=4)
            if need_ctx:
                yc = _diff_ctx_attention(qc, kc, vc, lam_p, sub_g, lambda_init=lambda_init)
        else:
            y = _na_attention(q, k, v, kc, vc, _na_bias_table(na_rpb[l // 2] * LOG2E))
            if need_ctx:
                yc = _na_ctx_attention(qc, kc, vc)
        prev = (y, z, g, w_out_bf, ln_g3, ln_b3, l, alpha)
        prev_c = (yc, zc, gc, w_out_bf, ln_g3, ln_b3, l, alpha) if need_ctx else None
    y, z, g, w_out_all, ln_g_all, ln_b_all, last = prev[:7]
    return _outproj(y, z, x, g, w_out_all, ln_g_all, ln_b_all, last, tm=512, alpha=alpha)
```
